```python
import math
import jax, jax.numpy as jnp
from jax import lax
import numpy as np

D_MODEL = 1024
BATCH = 16
SEQ = 2048
DEPTH = 2

RWKV_HEADS = 8
RWKV_HEAD_DIM = 64
D_RWKV = RWKV_HEADS * RWKV_HEAD_DIM
W_LORA = 64
A_LORA = 64
RWKV_LN_EPS = 64e-5
RET_HEADS = 8
RET_HEAD_DIM = 64
D_RET = RET_HEADS * RET_HEAD_DIM
RET_CHUNK = 128
ROPE_BASE = 10000.0
S5_GROUP = 16
S5_STATE = 64
D_S5 = 512
S5_GROUPS = D_S5 // S5_GROUP
DT_MIN = 1e-3
DT_MAX = 1e-1
N_BRANCH = 3
EPS = 1e-6

IN_SPLITS = (
    D_RWKV, D_RWKV, D_RWKV, W_LORA, A_LORA, D_RWKV,
    D_RET, D_RET, D_RET, D_RET,
    D_S5, D_S5,
    D_MODEL, D_MODEL, D_MODEL,
)
D_IN = 4 * D_RWKV + W_LORA + A_LORA + 4 * D_RET + 2 * D_S5 + N_BRANCH * D_MODEL

kernel_name = "hybrid_rwkv7_retnet_s5_gated_block"


def _rmsnorm(x, g):
    x32 = x.astype(jnp.float32)
    y = x32 * lax.rsqrt(jnp.mean(x32 * x32, axis=-1, keepdims=True) + EPS)
    return (y * g.astype(jnp.float32)).astype(x.dtype)


def _head_norm(y, eps):
    mu = jnp.mean(y, axis=-1, keepdims=True)
    var = jnp.mean(jnp.square(y - mu), axis=-1, keepdims=True)
    return (y - mu) * lax.rsqrt(var + eps)


def _split_cols(z):
    out, start = [], 0
    for size in IN_SPLITS:
        out.append(z[..., start:start + size])
        start += size
    return out


def _token_shift(z, mu):
    prev = jnp.pad(z[:, :-1], ((0, 0), (1, 0), (0, 0)))
    return z + mu * (prev - z)


def _rwkv7_branch(r, k, v, xw, xa, mu_rkv, mu_wa, w0, w2, a0, a2, k_k, k_a, r_k, ln_w, ln_b):
    f32 = jnp.float32
    bsz, seq, _ = r.shape
    r = _token_shift(r.astype(f32), mu_rkv[0])
    k = _token_shift(k.astype(f32), mu_rkv[1])
    v = _token_shift(v.astype(f32), mu_rkv[2])
    xw = _token_shift(xw.astype(f32), mu_wa[0])
    xa = _token_shift(xa.astype(f32), mu_wa[1])
    w_log = -jax.nn.softplus(-(w0 + jnp.tanh(xw) @ w2)) - 0.5
    decay = jnp.exp(-jnp.exp(w_log))
    a = jax.nn.sigmoid(a0 + xa @ a2)
    kk = k * k_k
    k = k * (1.0 + (a - 1.0) * k_a)
    hs = lambda t: t.reshape(bsz, seq, RWKV_HEADS, RWKV_HEAD_DIM)
    r, k, v, decay, a, kk = map(hs, (r, k, v, decay, a, kk))
    kk = kk * lax.rsqrt(jnp.sum(kk * kk, axis=-1, keepdims=True) + 1e-12)
    a_neg, b_vec = -kk, kk * a

    def step(state, inp):
        r_t, w_t, k_t, v_t, an_t, b_t = inp
        sa = jnp.einsum('bhvk,bhk->bhv', state, an_t)
        state = (state * w_t[:, :, None, :] + sa[..., None] * b_t[:, :, None, :]
                 + v_t[..., None] * k_t[:, :, None, :])
        return state, jnp.einsum('bhvk,bhk->bhv', state, r_t)

    tm = lambda t: jnp.moveaxis(t, 1, 0)
    s0 = jnp.zeros((bsz, RWKV_HEADS, RWKV_HEAD_DIM, RWKV_HEAD_DIM), f32)
    _, y = lax.scan(step, s0, (tm(r), tm(decay), tm(k), tm(v), tm(a_neg), tm(b_vec)))
    y = jnp.moveaxis(y, 0, 1)
    y = _head_norm(y, RWKV_LN_EPS) * ln_w.reshape(RWKV_HEADS, RWKV_HEAD_DIM) \
        + ln_b.reshape(RWKV_HEADS, RWKV_HEAD_DIM)
    y = y + jnp.sum(r * k * r_k, axis=-1, keepdims=True) * v
    return y.reshape(bsz, seq, D_RWKV)


def _rope(x, pos):
    half = x.shape[-1] // 2
    inv = ROPE_BASE ** (-jnp.arange(half, dtype=jnp.float32) / half)
    ang = pos[:, None] * inv[None, :]
    cos, sin = jnp.cos(ang)[None, :, None, :], jnp.sin(ang)[None, :, None, :]
    x1, x2 = x[..., :half], x[..., half:]
    return jnp.concatenate([x1 * cos - x2 * sin, x1 * sin + x2 * cos], axis=-1)


def _retention_branch(q, k, v):
    f32 = jnp.float32
    bsz, seq, _ = q.shape
    nc, C, H, dh = seq // RET_CHUNK, RET_CHUNK, RET_HEADS, RET_HEAD_DIM
    hs = lambda t: t.astype(f32).reshape(bsz, seq, H, dh)
    q, k, v = hs(q), hs(k), hs(v)
    pos = jnp.arange(seq, dtype=f32)
    q = _rope(q, pos)
    k = _rope(k, pos) * (dh ** -0.5)
    ch = lambda t: t.reshape(bsz, nc, C, H, dh)
    q, k, v = ch(q), ch(k), ch(v)
    log_gamma = jnp.log(1.0 - 2.0 ** (-5.0 - jnp.arange(H, dtype=f32)))
    idx = jnp.arange(C, dtype=f32)
    diff = idx[:, None] - idx[None, :]
    dmask = jnp.where(diff >= 0, jnp.exp(log_gamma[:, None, None] * jnp.maximum(diff, 0.0)), 0.0)
    scores = jnp.einsum('bnihd,bnjhd->bnhij', q, k) * dmask
    intra = jnp.einsum('bnhij,bnjhd->bnihd', scores, v)
    k_w = jnp.exp(log_gamma[:, None] * (C - 1.0 - idx)[None, :])
    kv = jnp.einsum('bnjhd,hj,bnjhe->bnhde', k, k_w, v)
    chunk_decay = jnp.exp(log_gamma * C)[:, None, None]

    def step(R, kv_n):
        return R * chunk_decay + kv_n, R

    _, r_prev = lax.scan(step, jnp.zeros((bsz, H, dh, dh), f32), jnp.moveaxis(kv, 1, 0))
    r_prev = jnp.moveaxis(r_prev, 0, 1)
    q_w = jnp.exp(log_gamma[:, None] * (idx + 1.0)[None, :])
    cross = jnp.einsum('bnihd,hi,bnhde->bnihe', q, q_w, r_prev)
    y = (intra + cross).reshape(bsz, seq, H, dh)
    return _head_norm(y, EPS).reshape(bsz, seq, D_RET)


def _s5_branch(u, A_re, A_im, log_dt, B_re, B_im, C_re, C_im, D_skip, glu_w, glu_b):
    f32 = jnp.float32
    bsz, seq, _ = u.shape
    u32 = u.astype(f32)
    ug = u32.reshape(bsz, seq, S5_GROUPS, S5_GROUP)
    dt = jnp.exp(log_dt)[:, None]
    mag = jnp.exp(dt * A_re)
    ang = dt * A_im
    ab_re, ab_im = mag * jnp.cos(ang), mag * jnp.sin(ang)
    p, q = ab_re - 1.0, ab_im
    den = A_re * A_re + A_im * A_im
    c_re = (p * A_re + q * A_im) / den
    c_im = (q * A_re - p * A_im) / den
    bb_re = c_re[..., None] * B_re - c_im[..., None] * B_im
    bb_im = c_re[..., None] * B_im + c_im[..., None] * B_re
    bu_re = jnp.einsum('bsgp,gnp->bsgn', ug, bb_re)
    bu_im = jnp.einsum('bsgp,gnp->bsgn', ug, bb_im)
    a_re = jnp.broadcast_to(ab_re[None, None], (1, seq, S5_GROUPS, S5_STATE))
    a_im = jnp.broadcast_to(ab_im[None, None], (1, seq, S5_GROUPS, S5_STATE))

    def combine(e1, e2):
        a1r, a1i, b1r, b1i = e1
        a2r, a2i, b2r, b2i = e2
        return (a2r * a1r - a2i * a1i, a2r * a1i + a2i * a1r,
                a2r * b1r - a2i * b1i + b2r, a2r * b1i + a2i * b1r + b2i)

    _, _, xr, xi = lax.associative_scan(combine, (a_re, a_im, bu_re, bu_im), axis=1)
    y = jnp.einsum('bsgn,gpn->bsgp', xr, C_re) - jnp.einsum('bsgn,gpn->bsgp', xi, C_im)
    y = y.reshape(bsz, seq, D_S5) + D_skip * u32
    z = jax.nn.gelu(y)
    return z * jax.nn.sigmoid(z @ glu_w + glu_b)


def setup_inputs(seed: int = 0) -> dict:
    key = jax.random.key(seed)
    ks = jax.random.split(key, 32)
    nrm = lambda i, shape, s: s * jax.random.normal(ks[i], shape, jnp.float32)
    L, G, N, P = DEPTH, S5_GROUPS, S5_STATE, S5_GROUP
    inv2 = 2.0 ** -0.5
    return {
        "x": nrm(0, (BATCH, SEQ, D_MODEL), 1.0),
        "pre_norm": 1.0 + nrm(1, (L, D_MODEL), 0.02),
        "w_in": nrm(2, (L, D_MODEL, D_IN), D_MODEL ** -0.5),
        "rwkv_mu_rkv": jax.random.uniform(ks[3], (L, 3, D_RWKV), jnp.float32),
        "rwkv_mu_wa": jax.random.uniform(ks[4], (L, 2, W_LORA), jnp.float32),
        "rwkv_w0": jnp.linspace(-6.0, -1.0, D_RWKV, dtype=jnp.float32)[None, :] + nrm(5, (L, D_RWKV), 0.1),
        "rwkv_w2": nrm(6, (L, W_LORA, D_RWKV), 0.5 * W_LORA ** -0.5),
        "rwkv_a0": nrm(7, (L, D_RWKV), 0.1),
        "rwkv_a2": nrm(8, (L, A_LORA, D_RWKV), 0.5 * A_LORA ** -0.5),
        "rwkv_k_k": 0.85 + nrm(9, (L, D_RWKV), 0.05),
        "rwkv_k_a": 1.0 + nrm(10, (L, D_RWKV), 0.05),
        "rwkv_r_k": nrm(11, (L, RWKV_HEADS, RWKV_HEAD_DIM), 0.1),
        "rwkv_ln_w": 1.0 + nrm(12, (L, D_RWKV), 0.02),
        "rwkv_ln_b": nrm(13, (L, D_RWKV), 0.01),
        "s5_A_re": -0.5 + nrm(14, (L, G, N), 0.01),
        "s5_A_im": jnp.broadcast_to(math.pi * jnp.arange(N, dtype=jnp.float32), (L, G, N)) + 0.0 * nrm(15, (L, G, N), 1.0) if False else jnp.broadcast_to(math.pi * jnp.arange(N, dtype=jnp.float32), (L, G, N)) * jnp.ones((L, G, N), jnp.float32),
        "s5_log_dt": jax.random.uniform(ks[16], (L, G), jnp.float32, math.log(DT_MIN), math.log(DT_MAX)),
        "s5_B_re": nrm(17, (L, G, N, P), inv2 * P ** -0.5),
        "s5_B_im": nrm(18, (L, G, N, P), inv2 * P ** -0.5),
        "s5_C_re": nrm(19, (L, G, P, N), inv2 * N ** -0.5),
        "s5_C_im": nrm(20, (L, G, P, N), inv2 * N ** -0.5),
        "s5_D": nrm(21, (L, D_S5), 1.0),
        "s5_glu_w": nrm(22, (L, D_S5, D_S5), D_S5 ** -0.5),
        "s5_glu_b": nrm(23, (L, D_S5), 0.01),
        "w_proj_rwkv": nrm(24, (L, D_RWKV, D_MODEL), D_RWKV ** -0.5),
        "w_proj_ret": nrm(25, (L, D_RET, D_MODEL), D_RET ** -0.5),
        "w_proj_s5": nrm(26, (L, D_S5, D_MODEL), D_S5 ** -0.5),
        "b_merge": nrm(27, (L, N_BRANCH, D_MODEL), 0.01),
        "w_out": nrm(28, (L, D_MODEL, D_MODEL), D_MODEL ** -0.5),
        "post_norm": 1.0 + nrm(29, (L, D_MODEL), 0.02),
    }


def reference(x, pre_norm, w_in, rwkv_mu_rkv, rwkv_mu_wa, rwkv_w0, rwkv_w2, rwkv_a0, rwkv_a2,
              rwkv_k_k, rwkv_k_a, rwkv_r_k, rwkv_ln_w, rwkv_ln_b, s5_A_re, s5_A_im, s5_log_dt,
              s5_B_re, s5_B_im, s5_C_re, s5_C_im, s5_D, s5_glu_w, s5_glu_b, w_proj_rwkv,
              w_proj_ret, w_proj_s5, b_merge, w_out, post_norm):
    f32 = jnp.float32
    for l in range(DEPTH):
        h = _rmsnorm(x, pre_norm[l])
        proj = h @ w_in[l]
        (a_r, a_k, a_v, a_xw, a_xa, a_gate,
         b_q, b_k, b_v, b_gate,
         c_u, c_gate,
         g_a, g_b, g_c) = _split_cols(proj)
        y_a = _rwkv7_branch(a_r, a_k, a_v, a_xw, a_xa, rwkv_mu_rkv[l], rwkv_mu_wa[l],
                            rwkv_w0[l], rwkv_w2[l], rwkv_a0[l], rwkv_a2[l], rwkv_k_k[l],
                            rwkv_k_a[l], rwkv_r_k[l], rwkv_ln_w[l], rwkv_ln_b[l])
        y_a = y_a * jax.nn.silu(a_gate.astype(f32))
        y_b = _retention_branch(b_q, b_k, b_v) * jax.nn.silu(b_gate.astype(f32))
        y_c = _s5_branch(c_u, s5_A_re[l], s5_A_im[l], s5_log_dt[l], s5_B_re[l], s5_B_im[l],
                         s5_C_re[l], s5_C_im[l], s5_D[l], s5_glu_w[l], s5_glu_b[l])
        y_c = y_c * jax.nn.silu(c_gate.astype(f32))
        merged = (jax.nn.sigmoid(g_a.astype(f32) + b_merge[l, 0]) * (y_a @ w_proj_rwkv[l])
                  + jax.nn.sigmoid(g_b.astype(f32) + b_merge[l, 1]) * (y_b @ w_proj_ret[l])
                  + jax.nn.sigmoid(g_c.astype(f32) + b_merge[l, 2]) * (y_c @ w_proj_s5[l]))
        out = (merged @ w_out[l]).astype(x.dtype)
        x = x + _rmsnorm(out, post_norm[l])
    return x
```

```python
import functools
import math

import jax
import jax.numpy as jnp
from jax import lax
from jax.experimental import pallas as pl
from jax.experimental.pallas import tpu as pltpu

F32 = jnp.float32
BF16 = jnp.bfloat16

D_MODEL = 1024
DEPTH = 2
HEAD_DIM = 64
D_RWKV = 512
W_LORA = 64
A_LORA = 64
RWKV_LN_EPS = 64e-5
D_RET = 512
RET_HEADS = 8
RET_CHUNK = 128
ROPE_BASE = 10000.0
S5_GROUP = 16
S5_STATE = 64
D_S5 = 512
S5_GROUPS = D_S5 // S5_GROUP
EPS = 1e-6

N_A = 4 * D_RWKV + W_LORA + A_LORA
N_B = 4 * D_RET
N_C = 2 * D_S5
N_G = 3 * D_MODEL
D_IN = N_A + N_B + N_C + N_G

RWKV_CHUNK = 64
RWKV_GROUP = 4
RET_GROUP = 2
S5_CHUNK = 32

VMEM_LIMIT = 56 * 1024 * 1024


def _bdot(a, b):
    return jnp.dot(a.astype(BF16), b.astype(BF16), preferred_element_type=F32)


def _bdot_nt(a, b):
    return lax.dot_general(a.astype(BF16), b.astype(BF16), (((1,), (1,)), ((), ())),
                           preferred_element_type=F32)


def _bdot_tn(a, b):
    return lax.dot_general(a.astype(BF16), b.astype(BF16), (((0,), (0,)), ((), ())),
                           preferred_element_type=F32)


def _dot_split(m, x):
    hi = x.astype(BF16)
    lo = (x - hi.astype(F32)).astype(BF16)
    return (jnp.dot(m, hi, preferred_element_type=F32)
            + jnp.dot(m, lo, preferred_element_type=F32))


def _sigmoid(x):
    return 1.0 / (1.0 + jnp.exp(-x))


def _silu(x):
    return x * _sigmoid(x)


def _tile_rows(x, n):
    return jnp.concatenate([x] * n, axis=0)


def _pick_heads(y, n, rows, lane_head):
    out = y[0:rows]
    for h in range(1, n):
        out = jnp.where(lane_head == h, y[h * rows:(h + 1) * rows], out)
    return out


def _in_proj_kernel(x_ref, g_ref, w_ref, oa_ref, ob_ref, oc_ref, og_ref):
    x = x_ref[...]
    h = (x * lax.rsqrt(jnp.mean(x * x, axis=-1, keepdims=True) + EPS) * g_ref[...]).astype(BF16)
    col = 0
    for o_ref in (oa_ref, ob_ref, oc_ref, og_ref):
        n = o_ref.shape[-1]
        for c0 in range(0, n, 512):
            c1 = min(c0 + 512, n)
            o_ref[:, c0:c1] = jnp.dot(h, w_ref[:, col + c0:col + c1],
                                      preferred_element_type=F32).astype(BF16)
        col += n


def _in_proj(x2, g, w, tm=256):
    m = x2.shape[0]
    widths = (N_A, N_B, N_C, N_G)
    return pl.pallas_call(
        _in_proj_kernel,
        out_shape=[jax.ShapeDtypeStruct((m, n), BF16) for n in widths],
        grid=(m // tm,),
        in_specs=[pl.BlockSpec((tm, D_MODEL), lambda i: (i, 0)),
                  pl.BlockSpec((1, D_MODEL), lambda i: (0, 0)),
                  pl.BlockSpec((D_MODEL, D_IN), lambda i: (0, 0))],
        out_specs=[pl.BlockSpec((tm, n), lambda i: (i, 0)) for n in widths],
        compiler_params=pltpu.CompilerParams(dimension_semantics=("arbitrary",),
                                             vmem_limit_bytes=VMEM_LIMIT),
        name="in_proj",
    )(x2, g, w)


def _rwkv_kernel(z_ref, mu_ref, w0_ref, w2_ref, a0_ref, a2_ref, kk_ref, ka_ref, rk_ref,
                 lnw_ref, lnb_ref, tri_ref, hsum_ref, strict_ref, incl_ref, bd_ref,
                 o_ref, s_ref, prev_ref):
    L, G, GW = RWKV_CHUNK, RWKV_GROUP, RWKV_GROUP * HEAD_DIM

    @pl.when(pl.program_id(1) == 0)
    def _():
        s_ref[...] = jnp.zeros_like(s_ref)
        prev_ref[...] = jnp.zeros_like(prev_ref)

    z = z_ref[...].astype(F32)
    row = lax.broadcasted_iota(jnp.int32, z.shape, 0)
    zp = jnp.where(row == 0, prev_ref[...], pltpu.roll(z, 1, 0))
    prev_ref[...] = z[L - 1:L, :]
    zs = z + mu_ref[...] * (zp - z)
    r = zs[:, 0:512]
    k = zs[:, 512:1024]
    v = zs[:, 1024:1536]
    xw = zs[:, 1536:1600]
    xa = zs[:, 1600:1664]
    gate = z[:, 1664:2176]

    t = -(w0_ref[...] + _bdot(jnp.tanh(xw), w2_ref[...]))
    softplus = jnp.maximum(t, 0.0) + jnp.log(1.0 + jnp.exp(-jnp.abs(t)))
    ld = -jnp.exp(-softplus - 0.5)
    a = _sigmoid(a0_ref[...] + _bdot(xa, a2_ref[...]))
    kk = k * kk_ref[...]
    k2 = k * (1.0 + (a - 1.0) * ka_ref[...])
    kk = kk * lax.rsqrt(_bdot(kk * kk, hsum_ref[...]) + 1e-12)
    b = kk * a

    cum = _dot_split(tri_ref[...], ld)
    cum_l = cum[L - 1:L, :]
    e_neg = jnp.exp(-cum)
    e_rem = jnp.exp(cum_l - cum)
    rt = r * jnp.exp(cum)
    at = -kk * jnp.exp(cum - ld)
    kt = k2 * e_neg
    bt = b * e_neg
    kh = k2 * e_rem
    bh = b * e_rem
    w_l = jnp.exp(cum_l)

    lane_head = lax.broadcasted_iota(jnp.int32, (L, GW), 1) // HEAD_DIM
    row_head = lax.broadcasted_iota(jnp.int32, (G * L, GW), 0) // L
    col_head = lax.broadcasted_iota(jnp.int32, (G * L, GW), 1) // HEAD_DIM
    own = row_head == col_head
    strict = strict_ref[...] != 0.0
    incl = incl_ref[...] != 0.0
    bd = bd_ref[...] != 0.0
    eye = jnp.where(incl, 1.0, 0.0) - jnp.where(strict, 1.0, 0.0)

    ys = []
    for g in range(D_RWKV // GW):
        sl = slice(g * GW, (g + 1) * GW)
        s0 = s_ref[g]
        lhs = jnp.concatenate([jnp.where(own, _tile_rows(at[:, sl], G), 0.0),
                               jnp.where(own, _tile_rows(rt[:, sl], G), 0.0)], axis=0)
        rhs = jnp.concatenate([_tile_rows(bt[:, sl], G), _tile_rows(kt[:, sl], G)], axis=0)
        gram = _bdot_nt(lhs, rhs)
        n = G * L
        m_ab = jnp.where(strict, gram[0:n, 0:n], 0.0)
        m_ak = jnp.where(strict, gram[0:n, n:2 * n], 0.0)
        m_rb = jnp.where(incl, gram[n:2 * n, 0:n], 0.0)
        m_rk = jnp.where(incl, gram[n:2 * n, n:2 * n], 0.0)
        p = m_ab
        tinv = eye + p
        for _ in range(5):
            p = _bdot(p, p)
            tinv = tinv + _bdot(p, tinv)
        vg = v[:, sl]
        v_t = _tile_rows(vg, G)
        x = _bdot_nt(at[:, sl], s0) + _pick_heads(_bdot(m_ak, v_t), G, L, lane_head)
        u = _pick_heads(_bdot(tinv, _tile_rows(x, G)), G, L, lane_head)
        y = _bdot_nt(rt[:, sl], s0) + _pick_heads(
            _bdot(m_rb, _tile_rows(u, G)) + _bdot(m_rk, v_t), G, L, lane_head)
        upd = _bdot_tn(jnp.concatenate([u, vg], axis=0),
                       jnp.concatenate([bh[:, sl], kh[:, sl]], axis=0))
        s_ref[g] = s0 * w_l[:, sl] + jnp.where(bd, upd, 0.0)
        ys.append(y)
    y = jnp.concatenate(ys, axis=1)

    hmean = hsum_ref[...] * (1.0 / HEAD_DIM)
    mean = _bdot(y, hmean)
    d = y - mean
    var = _bdot(d * d, hmean)
    yn = d * lax.rsqrt(var + RWKV_LN_EPS) * lnw_ref[...] + lnb_ref[...]
    bonus = _bdot(r * k2 * rk_ref[...], hsum_ref[...]) * v
    o_ref[...] = ((yn + bonus) * _silu(gate)).astype(BF16)


def _rwkv(za, batch, seq, p):
    L, gw = RWKV_CHUNK, RWKV_GROUP * HEAD_DIM
    n = RWKV_GROUP * L
    idx = jnp.arange(n)
    same = (idx[:, None] // L) == (idx[None, :] // L)
    strict = (same & ((idx[None, :] % L) < (idx[:, None] % L))).astype(F32)
    incl = (same & ((idx[None, :] % L) <= (idx[:, None] % L))).astype(F32)
    bd = ((idx[:, None] // HEAD_DIM) == (idx[None, :] // HEAD_DIM)).astype(F32)
    tri = (jnp.arange(L)[None, :] <= jnp.arange(L)[:, None]).astype(BF16)
    hid = jnp.arange(D_RWKV) // HEAD_DIM
    hsum = (hid[:, None] == hid[None, :]).astype(BF16)
    consts = (tri, hsum, strict, incl, bd)
    full = lambda a: pl.BlockSpec(a.shape, lambda b, c: (0,) * a.ndim)
    nc = seq // L
    return pl.pallas_call(
        _rwkv_kernel,
        out_shape=jax.ShapeDtypeStruct((batch * seq, D_RWKV), BF16),
        grid=(batch, nc),
        in_specs=[pl.BlockSpec((L, N_A), lambda b, c: (b * nc + c, 0))]
                 + [full(a) for a in p] + [full(a) for a in consts],
        out_specs=pl.BlockSpec((L, D_RWKV), lambda b, c: (b * nc + c, 0)),
        scratch_shapes=[pltpu.VMEM((D_RWKV // gw, gw, gw), F32),
                        pltpu.VMEM((1, N_A), F32)],
        compiler_params=pltpu.CompilerParams(dimension_semantics=("arbitrary", "arbitrary"),
                                             vmem_limit_bytes=VMEM_LIMIT),
        name="rwkv7",
    )(za, *p, *consts)


def _ret_kernel(z_ref, cos_ref, sin_ref, qw_ref, kw_ref, cd_ref, dmask_ref, hmean_ref, bd_ref,
                o_ref, s_ref):
    C, G, GW = RET_CHUNK, RET_GROUP, RET_GROUP * HEAD_DIM

    @pl.when(pl.program_id(1) == 0)
    def _():
        s_ref[...] = jnp.zeros_like(s_ref)

    z = z_ref[...].astype(F32)
    lane = lax.broadcasted_iota(jnp.int32, (C, D_RET), 1)
    first_half = (lane % HEAD_DIM) < (HEAD_DIM // 2)
    cos = cos_ref[...]
    sin = sin_ref[...]

    def rope(x):
        swapped = jnp.where(first_half, pltpu.roll(x, D_RET - HEAD_DIM // 2, 1),
                            pltpu.roll(x, HEAD_DIM // 2, 1))
        return x * cos + swapped * sin

    q = rope(z[:, 0:512])
    k = rope(z[:, 512:1024]) * (HEAD_DIM ** -0.5)
    v = z[:, 1024:1536]
    gate = z[:, 1536:2048]
    qw = q * qw_ref[...]
    kw = k * kw_ref[...]
    cd = cd_ref[...]

    own = (lax.broadcasted_iota(jnp.int32, (G * C, GW), 0) // C
           == lax.broadcasted_iota(jnp.int32, (G * C, GW), 1) // HEAD_DIM)
    lane_head = lax.broadcasted_iota(jnp.int32, (C, GW), 1) // HEAD_DIM
    bd = bd_ref[...] != 0.0
    ys = []
    for g in range(D_RET // GW):
        sl = slice(g * GW, (g + 1) * GW)
        s0 = s_ref[g]
        q_ms = jnp.where(own, _tile_rows(q[:, sl], G), 0.0)
        scores = _bdot_nt(q_ms, _tile_rows(k[:, sl], G)) * dmask_ref[g]
        intra = _pick_heads(_bdot(scores, _tile_rows(v[:, sl], G)), G, C, lane_head)
        ys.append(intra + _bdot(qw[:, sl], s0))
        s_ref[g] = s0 * cd[:, sl] + jnp.where(bd, _bdot_tn(kw[:, sl], v[:, sl]), 0.0)
    y = jnp.concatenate(ys, axis=1)

    hmean = hmean_ref[...]
    d = y - _bdot(y, hmean)
    var = _bdot(d * d, hmean)
    o_ref[...] = (d * lax.rsqrt(var + EPS) * _silu(gate)).astype(BF16)


def _retention(zb, batch, seq):
    C, gw = RET_CHUNK, RET_GROUP * HEAD_DIM
    half = HEAD_DIM // 2
    inv = ROPE_BASE ** (-jnp.arange(half, dtype=F32) / half)
    ang = jnp.arange(seq, dtype=F32)[:, None] * inv[None, :]
    cos = jnp.tile(jnp.cos(ang), (1, 2 * RET_HEADS))
    sin = jnp.tile(jnp.concatenate([-jnp.sin(ang), jnp.sin(ang)], axis=1), (1, RET_HEADS))
    log_gamma = jnp.log(1.0 - 2.0 ** (-5.0 - jnp.arange(RET_HEADS, dtype=F32)))
    idx = jnp.arange(C, dtype=F32)
    diff = idx[:, None] - idx[None, :]
    dmask = jnp.where(diff >= 0, jnp.exp(log_gamma[:, None, None] * jnp.maximum(diff, 0.0)), 0.0)
    ngr = RET_HEADS // RET_GROUP
    dm = jnp.zeros((ngr, RET_GROUP, C, RET_GROUP, C), F32)
    for j in range(RET_GROUP):
        dm = dm.at[:, j, :, j, :].set(dmask[j::RET_GROUP])
    dm = dm.reshape(ngr, RET_GROUP * C, RET_GROUP * C)
    per_lane = lambda t: jnp.repeat(t, HEAD_DIM, axis=-1)
    qw = per_lane(jnp.exp(log_gamma[None, :] * (idx + 1.0)[:, None]))
    kw = per_lane(jnp.exp(log_gamma[None, :] * (C - 1.0 - idx)[:, None]))
    cd = per_lane(jnp.exp(log_gamma * C)[None, :])
    hid = jnp.arange(D_RET) // HEAD_DIM
    hmean = ((hid[:, None] == hid[None, :]).astype(F32) / HEAD_DIM).astype(BF16)
    gid = jnp.arange(gw) // HEAD_DIM
    bd = (gid[:, None] == gid[None, :]).astype(F32)
    nc = seq // C
    full = lambda a: pl.BlockSpec(a.shape, lambda b, c: (0,) * a.ndim)
    return pl.pallas_call(
        _ret_kernel,
        out_shape=jax.ShapeDtypeStruct((batch * seq, D_RET), BF16),
        grid=(batch, nc),
        in_specs=[pl.BlockSpec((C, N_B), lambda b, c: (b * nc + c, 0)),
                  pl.BlockSpec((C, D_RET), lambda b, c: (c, 0)),
                  pl.BlockSpec((C, D_RET), lambda b, c: (c, 0)),
                  full(qw), full(kw), full(cd), full(dm), full(hmean), full(bd)],
        out_specs=pl.BlockSpec((C, D_RET), lambda b, c: (b * nc + c, 0)),
        scratch_shapes=[pltpu.VMEM((D_RET // gw, gw, gw), F32)],
        compiler_params=pltpu.CompilerParams(dimension_semantics=("arbitrary", "arbitrary"),
                                             vmem_limit_bytes=VMEM_LIMIT),
        name="retention",
    )(zb, cos, sin, qw, kw, cd, dm, hmean, bd)


def _s5_kernel(u_ref, tz_ref, so_ref, si_ref, lam_ref, o_ref, inc_ref, x0_ref, *, batch):
    u = u_ref[0]
    y = jnp.dot(u, tz_ref[0], preferred_element_type=F32)
    inc_ref[...] = jnp.dot(u, so_ref[0], preferred_element_type=F32)
    lam = lam_ref[0]
    lam_a = lam[0:1, :]
    lam_b = lam[1:2, :]
    nchunks = u.shape[0] // batch

    def step(c, x):
        rows = pl.ds(pl.multiple_of(c * batch, batch), batch)
        x0_ref[rows, :] = x
        swapped = pltpu.roll(x, S5_STATE, 1)
        return x * lam_a + swapped * lam_b + inc_ref[rows, :]

    lax.fori_loop(0, nchunks, step, jnp.zeros((batch, 2 * S5_STATE), F32))
    o_ref[0] = (y + _bdot(x0_ref[...], si_ref[0])).astype(BF16)


def _s5_tables(a_re, a_im, log_dt, b_re, b_im, c_re, c_im):
    lc, n, pch = S5_CHUNK, S5_STATE, S5_GROUP
    dt = jnp.exp(log_dt)[:, None]
    ab_re = jnp.exp(dt * a_re) * jnp.cos(dt * a_im)
    ab_im = jnp.exp(dt * a_re) * jnp.sin(dt * a_im)
    pr, qi = ab_re - 1.0, ab_im
    den = a_re * a_re + a_im * a_im
    cr = (pr * a_re + qi * a_im) / den
    ci = (qi * a_re - pr * a_im) / den
    bb_re = cr[..., None] * b_re - ci[..., None] * b_im
    bb_im = cr[..., None] * b_im + ci[..., None] * b_re
    j = jnp.arange(lc + 1, dtype=F32)[:, None, None]
    pw_mag = jnp.exp(j * (dt * a_re)[None])
    pw_re = pw_mag * jnp.cos(j * (dt * a_im)[None])
    pw_im = pw_mag * jnp.sin(j * (dt * a_im)[None])
    cl_re = c_re[None] * pw_re[:, :, None, :] - c_im[None] * pw_im[:, :, None, :]
    cl_im = c_re[None] * pw_im[:, :, None, :] + c_im[None] * pw_re[:, :, None, :]
    kern = (jnp.einsum('jgpn,gnq->jgpq', cl_re, bb_re)
            - jnp.einsum('jgpn,gnq->jgpq', cl_im, bb_im))
    s_idx = jnp.arange(lc)[:, None]
    t_idx = jnp.arange(lc)[None, :]
    lag = t_idx - s_idx
    tz = jnp.where((lag >= 0)[:, :, None, None, None], kern[jnp.maximum(lag, 0)], 0.0)
    tz = tz.transpose(2, 0, 4, 1, 3).reshape(S5_GROUPS, lc * pch, lc * pch)
    rev_re = pw_re[lc - 1::-1][:lc]
    rev_im = pw_im[lc - 1::-1][:lc]
    so_re = rev_re[..., None] * bb_re[None] - rev_im[..., None] * bb_im[None]
    so_im = rev_re[..., None] * bb_im[None] + rev_im[..., None] * bb_re[None]
    so = jnp.concatenate([so_re, so_im], axis=2)
    so = so.transpose(1, 0, 3, 2).reshape(S5_GROUPS, lc * pch, 2 * n)
    si = jnp.concatenate([cl_re[1:], -cl_im[1:]], axis=3)
    si = si.transpose(1, 3, 0, 2).reshape(S5_GROUPS, 2 * n, lc * pch)
    lam = jnp.stack([jnp.concatenate([pw_re[lc], pw_re[lc]], axis=-1),
                     jnp.concatenate([-pw_im[lc], pw_im[lc]], axis=-1)], axis=1)
    return tz.astype(BF16), so.astype(BF16), si.astype(BF16), lam


def _s5(u, batch, seq, tables):
    lc, pch = S5_CHUNK, S5_GROUP
    nc = seq // lc
    rows, width = nc * batch, lc * pch
    ug = u.reshape(batch, nc, lc, S5_GROUPS, pch).transpose(3, 1, 0, 2, 4).reshape(S5_GROUPS, rows, width)
    tz, so, si, lam = tables
    blk = lambda a: pl.BlockSpec((1,) + a.shape[1:], lambda g: (g, 0, 0))
    yg = pl.pallas_call(
        functools.partial(_s5_kernel, batch=batch),
        out_shape=jax.ShapeDtypeStruct((S5_GROUPS, rows, width), BF16),
        grid=(S5_GROUPS,),
        in_specs=[blk(ug), blk(tz), blk(so), blk(si), blk(lam)],
        out_specs=pl.BlockSpec((1, rows, width), lambda g: (g, 0, 0)),
        scratch_shapes=[pltpu.VMEM((rows, 2 * S5_STATE), F32),
                        pltpu.VMEM((rows, 2 * S5_STATE), F32)],
        compiler_params=pltpu.CompilerParams(dimension_semantics=("arbitrary",),
                                             vmem_limit_bytes=VMEM_LIMIT),
        name="s5",
    )(ug, tz, so, si, lam)
    return yg.reshape(S5_GROUPS, nc, batch, lc, pch).transpose(2, 1, 3, 0, 4).reshape(batch * seq, D_S5)


def _merge_kernel(ya_ref, yb_ref, ys_ref, zc_ref, zg_ref, x_ref, dskip_ref, gluw_ref, glub_ref,
                  wa_ref, wb_ref, wc_ref, bm_ref, wo_ref, pn_ref, o_ref):
    zc = zc_ref[...].astype(F32)
    y = ys_ref[...].astype(F32) + dskip_ref[...] * zc[:, 0:D_S5]
    gz = 0.5 * y * (1.0 + jnp.tanh(math.sqrt(2.0 / math.pi) * (y + 0.044715 * (y * y * y))))
    yc = gz * _sigmoid(_bdot(gz, gluw_ref[...]) + glub_ref[...]) * _silu(zc[:, D_S5:2 * D_S5])
    zg = zg_ref[...].astype(F32)
    bm = bm_ref[...]
    merged = (_sigmoid(zg[:, 0:D_MODEL] + bm[0:1, :])
              * jnp.dot(ya_ref[...], wa_ref[...], preferred_element_type=F32)
              + _sigmoid(zg[:, D_MODEL:2 * D_MODEL] + bm[1:2, :])
              * jnp.dot(yb_ref[...], wb_ref[...], preferred_element_type=F32)
              + _sigmoid(zg[:, 2 * D_MODEL:3 * D_MODEL] + bm[2:3, :]) * _bdot(yc, wc_ref[...]))
    out = _bdot(merged, wo_ref[...])
    normed = out * lax.rsqrt(jnp.mean(out * out, axis=-1, keepdims=True) + EPS) * pn_ref[...]
    o_ref[...] = x_ref[...] + normed


def _merge(ya, yb, ys, zc, zg, x2, params, tm=256):
    m = x2.shape[0]
    tok = lambda n: pl.BlockSpec((tm, n), lambda i: (i, 0))
    full = lambda a: pl.BlockSpec(a.shape, lambda i: (0,) * a.ndim)
    return pl.pallas_call(
        _merge_kernel,
        out_shape=jax.ShapeDtypeStruct((m, D_MODEL), F32),
        grid=(m // tm,),
        in_specs=[tok(D_RWKV), tok(D_RET), tok(D_S5), tok(N_C), tok(N_G), tok(D_MODEL)]
                 + [full(a) for a in params],
        out_specs=tok(D_MODEL),
        compiler_params=pltpu.CompilerParams(dimension_semantics=("arbitrary",),
                                             vmem_limit_bytes=VMEM_LIMIT),
        name="merge_out",
    )(ya, yb, ys, zc, zg, x2, *params)


def kernel(x, pre_norm, w_in, rwkv_mu_rkv, rwkv_mu_wa, rwkv_w0, rwkv_w2, rwkv_a0, rwkv_a2,
           rwkv_k_k, rwkv_k_a, rwkv_r_k, rwkv_ln_w, rwkv_ln_b, s5_A_re, s5_A_im, s5_log_dt,
           s5_B_re, s5_B_im, s5_C_re, s5_C_im, s5_D, s5_glu_w, s5_glu_b, w_proj_rwkv,
           w_proj_ret, w_proj_s5, b_merge, w_out, post_norm):
    batch, seq, _ = x.shape
    x2 = x.reshape(batch * seq, D_MODEL)
    row = lambda a: a.reshape(1, -1).astype(F32)
    for l in range(DEPTH):
        za, zb, zc, zg = _in_proj(x2, row(pre_norm[l]), w_in[l].astype(BF16))
        mu = jnp.concatenate([rwkv_mu_rkv[l].reshape(-1), rwkv_mu_wa[l].reshape(-1),
                              jnp.zeros((D_RWKV,), F32)]).reshape(1, N_A)
        rwkv_params = (mu, row(rwkv_w0[l]), rwkv_w2[l].astype(BF16), row(rwkv_a0[l]),
                       rwkv_a2[l].astype(BF16), row(rwkv_k_k[l]), row(rwkv_k_a[l]),
                       row(rwkv_r_k[l]), row(rwkv_ln_w[l]), row(rwkv_ln_b[l]))
        ya = _rwkv(za, batch, seq, rwkv_params)
        yb = _retention(zb, batch, seq)
        tables = _s5_tables(s5_A_re[l], s5_A_im[l], s5_log_dt[l], s5_B_re[l], s5_B_im[l],
                            s5_C_re[l], s5_C_im[l])
        ys = _s5(zc[:, 0:D_S5], batch, seq, tables)
        merge_params = (row(s5_D[l]), s5_glu_w[l].astype(BF16), row(s5_glu_b[l]),
                        w_proj_rwkv[l].astype(BF16), w_proj_ret[l].astype(BF16),
                        w_proj_s5[l].astype(BF16), b_merge[l].astype(F32),
                        w_out[l].astype(BF16), row(post_norm[l]))
        x2 = _merge(ya, yb, ys, zc, zg, x2, merge_params)
    return x2.reshape(batch, seq, D_MODEL)
```

```python
import functools
import math

import jax
import jax.numpy as jnp
from jax import lax
from jax.experimental import pallas as pl
from jax.experimental.pallas import tpu as pltpu

F32 = jnp.float32
BF16 = jnp.bfloat16

D_MODEL = 1024
DEPTH = 2
HEAD_DIM = 64
D_RWKV = 512
W_LORA = 64
A_LORA = 64
RWKV_LN_EPS = 64e-5
D_RET = 512
RET_HEADS = 8
RET_CHUNK = 128
ROPE_BASE = 10000.0
S5_GROUP = 16
S5_STATE = 64
D_S5 = 512
S5_GROUPS = D_S5 // S5_GROUP
EPS = 1e-6

N_A = 4 * D_RWKV + W_LORA + A_LORA
N_B = 4 * D_RET
N_C = 2 * D_S5
N_G = 3 * D_MODEL
D_IN = N_A + N_B + N_C + N_G

RWKV_CHUNK = 64
RWKV_BLOCK = 256
RWKV_GROUP = 4
RET_GROUP = 2
S5_CHUNK = 32

VMEM_LIMIT = 56 * 1024 * 1024


def _bdot(a, b):
    return jnp.dot(a.astype(BF16), b.astype(BF16), preferred_element_type=F32)


def _bdot_nt(a, b):
    return lax.dot_general(a.astype(BF16), b.astype(BF16), (((1,), (1,)), ((), ())),
                           preferred_element_type=F32)


def _bdot_tn(a, b):
    return lax.dot_general(a.astype(BF16), b.astype(BF16), (((0,), (0,)), ((), ())),
                           preferred_element_type=F32)


def _dot_split(m, x):
    hi = x.astype(BF16)
    lo = (x - hi.astype(F32)).astype(BF16)
    return (jnp.dot(m, hi, preferred_element_type=F32)
            + jnp.dot(m, lo, preferred_element_type=F32))


def _sigmoid(x):
    return 1.0 / (1.0 + jnp.exp(-x))


def _silu(x):
    return x * _sigmoid(x)


def _tile_rows(x, n):
    return jnp.concatenate([x] * n, axis=0)


def _pick_heads(y, n, rows, lane_head):
    out = y[0:rows]
    for h in range(1, n):
        out = jnp.where(lane_head == h, y[h * rows:(h + 1) * rows], out)
    return out


def _in_proj_kernel(x_ref, g_ref, w_ref, oa_ref, ob_ref, oc_ref, og_ref):
    x = x_ref[...]
    h = (x * lax.rsqrt(jnp.mean(x * x, axis=-1, keepdims=True) + EPS) * g_ref[...]).astype(BF16)
    col = 0
    for o_ref in (oa_ref, ob_ref, oc_ref, og_ref):
        n = o_ref.shape[-1]
        for c0 in range(0, n, 512):
            c1 = min(c0 + 512, n)
            o_ref[:, c0:c1] = jnp.dot(h, w_ref[:, col + c0:col + c1],
                                      preferred_element_type=F32).astype(BF16)
        col += n


def _in_proj(x2, g, w, tm=512):
    m = x2.shape[0]
    tm = min(tm, m)
    widths = (N_A, N_B, N_C, N_G)
    return pl.pallas_call(
        _in_proj_kernel,
        out_shape=[jax.ShapeDtypeStruct((m, n), BF16) for n in widths],
        grid=(m // tm,),
        in_specs=[pl.BlockSpec((tm, D_MODEL), lambda i: (i, 0)),
                  pl.BlockSpec((1, D_MODEL), lambda i: (0, 0)),
                  pl.BlockSpec((D_MODEL, D_IN), lambda i: (0, 0), pipeline_mode=pl.Buffered(1))],
        out_specs=[pl.BlockSpec((tm, n), lambda i: (i, 0)) for n in widths],
        compiler_params=pltpu.CompilerParams(dimension_semantics=("arbitrary",),
                                             vmem_limit_bytes=VMEM_LIMIT),
        name="in_proj",
    )(x2, g, w)


def _rwkv_kernel(z_ref, mu_ref, w0_ref, w2_ref, a0_ref, a2_ref, kk_ref, ka_ref, rk_ref,
                 lnw_ref, lnb_ref, tri_ref, ones_ref, hsum_ref, strict_ref, incl_ref, bd_ref,
                 o_ref, s_ref, prev_ref, y_ref):
    L, G, GW, NB = RWKV_CHUNK, RWKV_GROUP, RWKV_GROUP * HEAD_DIM, RWKV_BLOCK

    @pl.when(pl.program_id(1) == 0)
    def _():
        s_ref[...] = jnp.zeros_like(s_ref)
        prev_ref[...] = jnp.zeros_like(prev_ref)

    z = z_ref[...].astype(F32)
    row = lax.broadcasted_iota(jnp.int32, z.shape, 0)
    zp = jnp.where(row == 0, prev_ref[...], pltpu.roll(z, 1, 0))
    prev_ref[...] = z[NB - 1:NB, :]
    zs = z + mu_ref[...] * (zp - z)
    r = zs[:, 0:512]
    k = zs[:, 512:1024]
    v = zs[:, 1024:1536]
    xw = zs[:, 1536:1600]
    xa = zs[:, 1600:1664]
    gate = z[:, 1664:2176]

    t = -(w0_ref[...] + _bdot(jnp.tanh(xw), w2_ref[...]))
    softplus = jnp.maximum(t, 0.0) + jnp.log(1.0 + jnp.exp(-jnp.abs(t)))
    ld = -jnp.exp(-softplus - 0.5)
    a = _sigmoid(a0_ref[...] + _bdot(xa, a2_ref[...]))
    kk = k * kk_ref[...]
    k2 = k * (1.0 + (a - 1.0) * ka_ref[...])
    kk = kk * lax.rsqrt(_bdot(kk * kk, hsum_ref[...]) + 1e-12)
    b = kk * a

    cum = _dot_split(tri_ref[...], ld)
    tot = _dot_split(ones_ref[...], ld)
    e_neg = jnp.exp(-cum)
    e_rem = jnp.exp(tot - cum)
    rt = r * jnp.exp(cum)
    at = -kk * jnp.exp(cum - ld)
    kt = k2 * e_neg
    bt = b * e_neg
    kh = k2 * e_rem
    bh = b * e_rem
    w_tot = jnp.exp(tot)

    lane_head = lax.broadcasted_iota(jnp.int32, (L, GW), 1) // HEAD_DIM
    own = (lax.broadcasted_iota(jnp.int32, (G * L, GW), 0) // L
           == lax.broadcasted_iota(jnp.int32, (G * L, GW), 1) // HEAD_DIM)
    strict = strict_ref[...] != 0.0
    incl = incl_ref[...] != 0.0
    bd = bd_ref[...] != 0.0
    eye = jnp.where(incl, 1.0, 0.0) - jnp.where(strict, 1.0, 0.0)
    pick = lambda m: _pick_heads(m, G, L, lane_head)
    n = G * L

    for g in range(D_RWKV // GW):
        sl = slice(g * GW, (g + 1) * GW)
        s = s_ref[g]
        for c in range(NB // L):
            rs = slice(c * L, (c + 1) * L)
            at_c, rt_c, v_c = at[rs, sl], rt[rs, sl], v[rs, sl]
            lhs = jnp.concatenate([jnp.where(own, _tile_rows(at_c, G), 0.0),
                                   jnp.where(own, _tile_rows(rt_c, G), 0.0)], axis=0)
            rhs = jnp.concatenate([_tile_rows(bt[rs, sl], G), _tile_rows(kt[rs, sl], G)], axis=0)
            gram = _bdot_nt(lhs, rhs)
            m_ab = jnp.where(strict, gram[0:n, 0:n], 0.0)
            m_ak = jnp.where(strict, gram[0:n, n:2 * n], 0.0)
            m_rb = jnp.where(incl, gram[n:2 * n, 0:n], 0.0)
            m_rk = jnp.where(incl, gram[n:2 * n, n:2 * n], 0.0)
            p = m_ab
            tinv = eye + p
            for _ in range(5):
                p = _bdot(p, p)
                tinv = tinv + _bdot(p, tinv)
            v_t = _tile_rows(v_c, G)
            ta = pick(_bdot(tinv, _tile_rows(at_c, G)))
            tmv = pick(_bdot(tinv, _tile_rows(pick(_bdot(m_ak, v_t)), G)))
            ra = rt_c + pick(_bdot(m_rb, _tile_rows(ta, G)))
            y0 = pick(_bdot(m_rb, _tile_rows(tmv, G)) + _bdot(m_rk, v_t))
            bh_c = bh[rs, sl]
            q = jnp.where(bd, _bdot_tn(ta, bh_c), 0.0)
            cmat = jnp.where(bd, _bdot_tn(jnp.concatenate([tmv, v_c], axis=0),
                                          jnp.concatenate([bh_c, kh[rs, sl]], axis=0)), 0.0)
            y_ref[rs, sl] = _bdot_nt(ra, s) + y0
            s = s * w_tot[c * L:c * L + 1, sl] + _bdot(s, q) + cmat
        s_ref[g] = s
    y = y_ref[...]

    hmean = hsum_ref[...] * (1.0 / HEAD_DIM)
    mean = _bdot(y, hmean)
    d = y - mean
    var = _bdot(d * d, hmean)
    yn = d * lax.rsqrt(var + RWKV_LN_EPS) * lnw_ref[...] + lnb_ref[...]
    bonus = _bdot(r * k2 * rk_ref[...], hsum_ref[...]) * v
    o_ref[...] = ((yn + bonus) * _silu(gate)).astype(BF16)


def _rwkv(za, batch, seq, p):
    L, gw, nb = RWKV_CHUNK, RWKV_GROUP * HEAD_DIM, RWKV_BLOCK
    n = RWKV_GROUP * L
    idx = jnp.arange(n)
    same = (idx[:, None] // L) == (idx[None, :] // L)
    strict = (same & ((idx[None, :] % L) < (idx[:, None] % L))).astype(F32)
    incl = (same & ((idx[None, :] % L) <= (idx[:, None] % L))).astype(F32)
    bd = ((idx[:, None] // HEAD_DIM) == (idx[None, :] // HEAD_DIM)).astype(F32)
    bidx = jnp.arange(nb)
    chunk_same = (bidx[:, None] // L) == (bidx[None, :] // L)
    tri = (chunk_same & (bidx[None, :] <= bidx[:, None])).astype(BF16)
    ones = chunk_same.astype(BF16)
    hid = jnp.arange(D_RWKV) // HEAD_DIM
    hsum = (hid[:, None] == hid[None, :]).astype(BF16)
    consts = (tri, ones, hsum, strict, incl, bd)
    full = lambda a: pl.BlockSpec(a.shape, lambda b, c: (0,) * a.ndim)
    nblk = seq // nb
    return pl.pallas_call(
        _rwkv_kernel,
        out_shape=jax.ShapeDtypeStruct((batch * seq, D_RWKV), BF16),
        grid=(batch, nblk),
        in_specs=[pl.BlockSpec((nb, N_A), lambda b, c: (b * nblk + c, 0))]
                 + [full(a) for a in p] + [full(a) for a in consts],
        out_specs=pl.BlockSpec((nb, D_RWKV), lambda b, c: (b * nblk + c, 0)),
        scratch_shapes=[pltpu.VMEM((D_RWKV // gw, gw, gw), F32),
                        pltpu.VMEM((1, N_A), F32),
                        pltpu.VMEM((nb, D_RWKV), F32)],
        compiler_params=pltpu.CompilerParams(dimension_semantics=("arbitrary", "arbitrary"),
                                             vmem_limit_bytes=VMEM_LIMIT),
        name="rwkv7",
    )(za, *p, *consts)


def _ret_kernel(z_ref, cos_ref, sin_ref, qw_ref, kw_ref, cd_ref, dmask_ref, hmean_ref, bd_ref,
                o_ref, s_ref):
    C, G, GW = RET_CHUNK, RET_GROUP, RET_GROUP * HEAD_DIM

    @pl.when(pl.program_id(1) == 0)
    def _():
        s_ref[...] = jnp.zeros_like(s_ref)

    z = z_ref[...].astype(F32)
    lane = lax.broadcasted_iota(jnp.int32, (C, D_RET), 1)
    first_half = (lane % HEAD_DIM) < (HEAD_DIM // 2)
    cos = cos_ref[...]
    sin = sin_ref[...]

    def rope(x):
        swapped = jnp.where(first_half, pltpu.roll(x, D_RET - HEAD_DIM // 2, 1),
                            pltpu.roll(x, HEAD_DIM // 2, 1))
        return x * cos + swapped * sin

    q = rope(z[:, 0:512])
    k = rope(z[:, 512:1024]) * (HEAD_DIM ** -0.5)
    v = z[:, 1024:1536]
    gate = z[:, 1536:2048]
    qw = q * qw_ref[...]
    kw = k * kw_ref[...]
    cd = cd_ref[...]

    own = (lax.broadcasted_iota(jnp.int32, (G * C, GW), 0) // C
           == lax.broadcasted_iota(jnp.int32, (G * C, GW), 1) // HEAD_DIM)
    lane_head = lax.broadcasted_iota(jnp.int32, (C, GW), 1) // HEAD_DIM
    bd = bd_ref[...] != 0.0
    ys = []
    for g in range(D_RET // GW):
        sl = slice(g * GW, (g + 1) * GW)
        s0 = s_ref[g]
        q_ms = jnp.where(own, _tile_rows(q[:, sl], G), 0.0)
        scores = _bdot_nt(q_ms, _tile_rows(k[:, sl], G)) * dmask_ref[g]
        intra = _pick_heads(_bdot(scores, _tile_rows(v[:, sl], G)), G, C, lane_head)
        ys.append(intra + _bdot(qw[:, sl], s0))
        s_ref[g] = s0 * cd[:, sl] + jnp.where(bd, _bdot_tn(kw[:, sl], v[:, sl]), 0.0)
    y = jnp.concatenate(ys, axis=1)

    hmean = hmean_ref[...]
    d = y - _bdot(y, hmean)
    var = _bdot(d * d, hmean)
    o_ref[...] = (d * lax.rsqrt(var + EPS) * _silu(gate)).astype(BF16)


def _retention(zb, batch, seq):
    C, gw = RET_CHUNK, RET_GROUP * HEAD_DIM
    half = HEAD_DIM // 2
    inv = ROPE_BASE ** (-jnp.arange(half, dtype=F32) / half)
    ang = jnp.arange(seq, dtype=F32)[:, None] * inv[None, :]
    cos = jnp.tile(jnp.cos(ang), (1, 2 * RET_HEADS))
    sin = jnp.tile(jnp.concatenate([-jnp.sin(ang), jnp.sin(ang)], axis=1), (1, RET_HEADS))
    log_gamma = jnp.log(1.0 - 2.0 ** (-5.0 - jnp.arange(RET_HEADS, dtype=F32)))
    idx = jnp.arange(C, dtype=F32)
    diff = idx[:, None] - idx[None, :]
    dmask = jnp.where(diff >= 0, jnp.exp(log_gamma[:, None, None] * jnp.maximum(diff, 0.0)), 0.0)
    ngr = RET_HEADS // RET_GROUP
    dm = jnp.zeros((ngr, RET_GROUP, C, RET_GROUP, C), F32)
    for j in range(RET_GROUP):
        dm = dm.at[:, j, :, j, :].set(dmask[j::RET_GROUP])
    dm = dm.reshape(ngr, RET_GROUP * C, RET_GROUP * C)
    per_lane = lambda t: jnp.repeat(t, HEAD_DIM, axis=-1)
    qw = per_lane(jnp.exp(log_gamma[None, :] * (idx + 1.0)[:, None]))
    kw = per_lane(jnp.exp(log_gamma[None, :] * (C - 1.0 - idx)[:, None]))
    cd = per_lane(jnp.exp(log_gamma * C)[None, :])
    hid = jnp.arange(D_RET) // HEAD_DIM
    hmean = ((hid[:, None] == hid[None, :]).astype(F32) / HEAD_DIM).astype(BF16)
    gid = jnp.arange(gw) // HEAD_DIM
    bd = (gid[:, None] == gid[None, :]).astype(F32)
    nc = seq // C
    full = lambda a: pl.BlockSpec(a.shape, lambda b, c: (0,) * a.ndim)
    return pl.pallas_call(
        _ret_kernel,
        out_shape=jax.ShapeDtypeStruct((batch * seq, D_RET), BF16),
        grid=(batch, nc),
        in_specs=[pl.BlockSpec((C, N_B), lambda b, c: (b * nc + c, 0)),
                  pl.BlockSpec((C, D_RET), lambda b, c: (c, 0)),
                  pl.BlockSpec((C, D_RET), lambda b, c: (c, 0)),
                  full(qw), full(kw), full(cd), full(dm), full(hmean), full(bd)],
        out_specs=pl.BlockSpec((C, D_RET), lambda b, c: (b * nc + c, 0)),
        scratch_shapes=[pltpu.VMEM((D_RET // gw, gw, gw), F32)],
        compiler_params=pltpu.CompilerParams(dimension_semantics=("arbitrary", "arbitrary"),
                                             vmem_limit_bytes=VMEM_LIMIT),
        name="retention",
    )(zb, cos, sin, qw, kw, cd, dm, hmean, bd)


def _s5_kernel(u_ref, tz_ref, so_ref, si_ref, lam_ref, o_ref, inc_ref, x0_ref, *, batch):
    u = u_ref[0]
    y = jnp.dot(u, tz_ref[0], preferred_element_type=F32)
    inc_ref[...] = jnp.dot(u, so_ref[0], preferred_element_type=F32)
    lam = lam_ref[0]
    lam_a = lam[0:1, :]
    lam_b = lam[1:2, :]
    nchunks = u.shape[0] // batch

    def step(c, x):
        rows = pl.ds(pl.multiple_of(c * batch, batch), batch)
        x0_ref[rows, :] = x
        swapped = pltpu.roll(x, S5_STATE, 1)
        return x * lam_a + swapped * lam_b + inc_ref[rows, :]

    lax.fori_loop(0, nchunks, step, jnp.zeros((batch, 2 * S5_STATE), F32))
    o_ref[0] = (y + _bdot(x0_ref[...], si_ref[0])).astype(BF16)


def _s5_tables(a_re, a_im, log_dt, b_re, b_im, c_re, c_im):
    lc, n, pch = S5_CHUNK, S5_STATE, S5_GROUP
    dt = jnp.exp(log_dt)[:, None]
    ab_re = jnp.exp(dt * a_re) * jnp.cos(dt * a_im)
    ab_im = jnp.exp(dt * a_re) * jnp.sin(dt * a_im)
    pr, qi = ab_re - 1.0, ab_im
    den = a_re * a_re + a_im * a_im
    cr = (pr * a_re + qi * a_im) / den
    ci = (qi * a_re - pr * a_im) / den
    bb_re = cr[..., None] * b_re - ci[..., None] * b_im
    bb_im = cr[..., None] * b_im + ci[..., None] * b_re
    j = jnp.arange(lc + 1, dtype=F32)[:, None, None]
    pw_mag = jnp.exp(j * (dt * a_re)[None])
    pw_re = pw_mag * jnp.cos(j * (dt * a_im)[None])
    pw_im = pw_mag * jnp.sin(j * (dt * a_im)[None])
    cl_re = c_re[None] * pw_re[:, :, None, :] - c_im[None] * pw_im[:, :, None, :]
    cl_im = c_re[None] * pw_im[:, :, None, :] + c_im[None] * pw_re[:, :, None, :]
    kern = (jnp.einsum('jgpn,gnq->jgpq', cl_re, bb_re)
            - jnp.einsum('jgpn,gnq->jgpq', cl_im, bb_im))
    s_idx = jnp.arange(lc)[:, None]
    t_idx = jnp.arange(lc)[None, :]
    lag = t_idx - s_idx
    tz = jnp.where((lag >= 0)[:, :, None, None, None], kern[jnp.maximum(lag, 0)], 0.0)
    tz = tz.transpose(2, 0, 4, 1, 3).reshape(S5_GROUPS, lc * pch, lc * pch)
    rev_re = pw_re[lc - 1::-1][:lc]
    rev_im = pw_im[lc - 1::-1][:lc]
    so_re = rev_re[..., None] * bb_re[None] - rev_im[..., None] * bb_im[None]
    so_im = rev_re[..., None] * bb_im[None] + rev_im[..., None] * bb_re[None]
    so = jnp.concatenate([so_re, so_im], axis=2)
    so = so.transpose(1, 0, 3, 2).reshape(S5_GROUPS, lc * pch, 2 * n)
    si = jnp.concatenate([cl_re[1:], -cl_im[1:]], axis=3)
    si = si.transpose(1, 3, 0, 2).reshape(S5_GROUPS, 2 * n, lc * pch)
    lam = jnp.stack([jnp.concatenate([pw_re[lc], pw_re[lc]], axis=-1),
                     jnp.concatenate([-pw_im[lc], pw_im[lc]], axis=-1)], axis=1)
    return tz.astype(BF16), so.astype(BF16), si.astype(BF16), lam


def _s5(u, batch, seq, tables):
    lc, pch = S5_CHUNK, S5_GROUP
    nc = seq // lc
    rows, width = nc * batch, lc * pch
    ug = u.reshape(batch, nc, lc, S5_GROUPS, pch).transpose(3, 1, 0, 2, 4).reshape(S5_GROUPS, rows, width)
    tz, so, si, lam = tables
    blk = lambda a: pl.BlockSpec((1,) + a.shape[1:], lambda g: (g, 0, 0))
    yg = pl.pallas_call(
        functools.partial(_s5_kernel, batch=batch),
        out_shape=jax.ShapeDtypeStruct((S5_GROUPS, rows, width), BF16),
        grid=(S5_GROUPS,),
        in_specs=[blk(ug), blk(tz), blk(so), blk(si), blk(lam)],
        out_specs=pl.BlockSpec((1, rows, width), lambda g: (g, 0, 0)),
        scratch_shapes=[pltpu.VMEM((rows, 2 * S5_STATE), F32),
                        pltpu.VMEM((rows, 2 * S5_STATE), F32)],
        compiler_params=pltpu.CompilerParams(dimension_semantics=("arbitrary",),
                                             vmem_limit_bytes=VMEM_LIMIT),
        name="s5",
    )(ug, tz, so, si, lam)
    return yg.reshape(S5_GROUPS, nc, batch, lc, pch).transpose(2, 1, 3, 0, 4).reshape(batch * seq, D_S5)


def _merge_kernel(ya_ref, yb_ref, ys_ref, zc_ref, zg_ref, x_ref, dskip_ref, gluw_ref, glub_ref,
                  wa_ref, wb_ref, wc_ref, bm_ref, wo_ref, pn_ref, o_ref):
    zc = zc_ref[...].astype(F32)
    y = ys_ref[...].astype(F32) + dskip_ref[...] * zc[:, 0:D_S5]
    gz = 0.5 * y * (1.0 + jnp.tanh(math.sqrt(2.0 / math.pi) * (y + 0.044715 * (y * y * y))))
    yc = gz * _sigmoid(_bdot(gz, gluw_ref[...]) + glub_ref[...]) * _silu(zc[:, D_S5:2 * D_S5])
    zg = zg_ref[...].astype(F32)
    bm = bm_ref[...]
    merged = (_sigmoid(zg[:, 0:D_MODEL] + bm[0:1, :])
              * jnp.dot(ya_ref[...], wa_ref[...], preferred_element_type=F32)
              + _sigmoid(zg[:, D_MODEL:2 * D_MODEL] + bm[1:2, :])
              * jnp.dot(yb_ref[...], wb_ref[...], preferred_element_type=F32)
              + _sigmoid(zg[:, 2 * D_MODEL:3 * D_MODEL] + bm[2:3, :]) * _bdot(yc, wc_ref[...]))
    out = _bdot(merged, wo_ref[...])
    normed = out * lax.rsqrt(jnp.mean(out * out, axis=-1, keepdims=True) + EPS) * pn_ref[...]
    o_ref[...] = x_ref[...] + normed


def _merge(ya, yb, ys, zc, zg, x2, params, tm=256):
    m = x2.shape[0]
    tok = lambda n: pl.BlockSpec((tm, n), lambda i: (i, 0))
    full = lambda a: pl.BlockSpec(a.shape, lambda i: (0,) * a.ndim)
    return pl.pallas_call(
        _merge_kernel,
        out_shape=jax.ShapeDtypeStruct((m, D_MODEL), F32),
        grid=(m // tm,),
        in_specs=[tok(D_RWKV), tok(D_RET), tok(D_S5), tok(N_C), tok(N_G), tok(D_MODEL)]
                 + [full(a) for a in params],
        out_specs=tok(D_MODEL),
        compiler_params=pltpu.CompilerParams(dimension_semantics=("arbitrary",),
                                             vmem_limit_bytes=VMEM_LIMIT),
        name="merge_out",
    )(ya, yb, ys, zc, zg, x2, *params)


def kernel(x, pre_norm, w_in, rwkv_mu_rkv, rwkv_mu_wa, rwkv_w0, rwkv_w2, rwkv_a0, rwkv_a2,
           rwkv_k_k, rwkv_k_a, rwkv_r_k, rwkv_ln_w, rwkv_ln_b, s5_A_re, s5_A_im, s5_log_dt,
           s5_B_re, s5_B_im, s5_C_re, s5_C_im, s5_D, s5_glu_w, s5_glu_b, w_proj_rwkv,
           w_proj_ret, w_proj_s5, b_merge, w_out, post_norm):
    batch, seq, _ = x.shape
    x2 = x.reshape(batch * seq, D_MODEL)
    row = lambda a: a.reshape(1, -1).astype(F32)
    for l in range(DEPTH):
        za, zb, zc, zg = _in_proj(x2, row(pre_norm[l]), w_in[l].astype(BF16))
        mu = jnp.concatenate([rwkv_mu_rkv[l].reshape(-1), rwkv_mu_wa[l].reshape(-1),
                              jnp.zeros((D_RWKV,), F32)]).reshape(1, N_A)
        rwkv_params = (mu, row(rwkv_w0[l]), rwkv_w2[l].astype(BF16), row(rwkv_a0[l]),
                       rwkv_a2[l].astype(BF16), row(rwkv_k_k[l]), row(rwkv_k_a[l]),
                       row(rwkv_r_k[l]), row(rwkv_ln_w[l]), row(rwkv_ln_b[l]))
        ya = _rwkv(za, batch, seq, rwkv_params)
        yb = _retention(zb, batch, seq)
        tables = _s5_tables(s5_A_re[l], s5_A_im[l], s5_log_dt[l], s5_B_re[l], s5_B_im[l],
                            s5_C_re[l], s5_C_im[l])
        ys = _s5(zc[:, 0:D_S5], batch, seq, tables)
        merge_params = (row(s5_D[l]), s5_glu_w[l].astype(BF16), row(s5_glu_b[l]),
                        w_proj_rwkv[l].astype(BF16), w_proj_ret[l].astype(BF16),
                        w_proj_s5[l].astype(BF16), b_merge[l].astype(F32),
                        w_out[l].astype(BF16), row(post_norm[l]))
        x2 = _merge(ya, yb, ys, zc, zg, x2, merge_params)
    return x2.reshape(batch, seq, D_MODEL)
```

```python
import functools
import math

import jax
import jax.numpy as jnp
from jax import lax
from jax.experimental import pallas as pl
from jax.experimental.pallas import tpu as pltpu

F32 = jnp.float32
BF16 = jnp.bfloat16

D_MODEL = 1024
DEPTH = 2
HEAD_DIM = 64
D_RWKV = 512
W_LORA = 64
A_LORA = 64
RWKV_LN_EPS = 64e-5
D_RET = 512
RET_HEADS = 8
RET_CHUNK = 128
ROPE_BASE = 10000.0
S5_GROUP = 16
S5_STATE = 64
D_S5 = 512
S5_GROUPS = D_S5 // S5_GROUP
EPS = 1e-6

N_A = 4 * D_RWKV + W_LORA + A_LORA
N_B = 4 * D_RET
N_C = 2 * D_S5
N_G = 3 * D_MODEL
D_IN = N_A + N_B + N_C + N_G

RWKV_CHUNK = 64
RWKV_BLOCK = 256
RWKV_GROUP = 4
RET_GROUP = 2
S5_CHUNK = 32
S5_SEQS = 2
S5_GROUPS_PER_STEP = 4
S5_PITCH = 40

VMEM_LIMIT = 56 * 1024 * 1024
LANES = 128


def _bdot(a, b):
    return jnp.dot(a.astype(BF16), b.astype(BF16), preferred_element_type=F32)


def _bdot_nt(a, b):
    return lax.dot_general(a.astype(BF16), b.astype(BF16), (((1,), (1,)), ((), ())),
                           preferred_element_type=F32)


def _bdot_tn(a, b):
    return lax.dot_general(a.astype(BF16), b.astype(BF16), (((0,), (0,)), ((), ())),
                           preferred_element_type=F32)


def _dot_split(m, x):
    hi = x.astype(BF16)
    lo = (x - hi.astype(F32)).astype(BF16)
    return (jnp.dot(m, hi, preferred_element_type=F32)
            + jnp.dot(m, lo, preferred_element_type=F32))


def _sigmoid(x):
    return 1.0 / (1.0 + jnp.exp(-x))


def _silu(x):
    return x * _sigmoid(x)


def _block_diag(x, own):
    x = x.astype(BF16)
    n = own.shape[0] // x.shape[0]
    return jnp.where(own, jnp.concatenate([x] * n, axis=0), jnp.zeros((), BF16))


def _in_proj_kernel(x_ref, g_ref, w_ref, oa_ref, ob_ref, oc_ref, og_ref):
    x = x_ref[...]
    h = (x * lax.rsqrt(jnp.mean(x * x, axis=-1, keepdims=True) + EPS) * g_ref[...]).astype(BF16)
    col = 0
    for o_ref in (oa_ref, ob_ref, oc_ref, og_ref):
        n = o_ref.shape[-1]
        for c0 in range(0, n, 512):
            c1 = min(c0 + 512, n)
            o_ref[:, c0:c1] = jnp.dot(h, w_ref[:, col + c0:col + c1],
                                      preferred_element_type=F32).astype(BF16)
        col += n


def _in_proj(x2, g, w, tm=512):
    m = x2.shape[0]
    tm = min(tm, m)
    widths = (N_A, N_B, N_C, N_G)
    return pl.pallas_call(
        _in_proj_kernel,
        out_shape=[jax.ShapeDtypeStruct((m, n), BF16) for n in widths],
        grid=(m // tm,),
        in_specs=[pl.BlockSpec((tm, D_MODEL), lambda i: (i, 0)),
                  pl.BlockSpec((1, D_MODEL), lambda i: (0, 0)),
                  pl.BlockSpec((D_MODEL, D_IN), lambda i: (0, 0), pipeline_mode=pl.Buffered(1))],
        out_specs=[pl.BlockSpec((tm, n), lambda i: (i, 0)) for n in widths],
        compiler_params=pltpu.CompilerParams(dimension_semantics=("arbitrary",),
                                             vmem_limit_bytes=VMEM_LIMIT),
        name="in_proj",
    )(x2, g, w)


def _rwkv_kernel(z_ref, mu_ref, w0_ref, w2_ref, a0_ref, a2_ref, kk_ref, ka_ref, rk_ref,
                 lnw_ref, lnb_ref, tri_ref, ones_ref, hsum_ref, own_ref, bd_ref,
                 o_ref, s_ref, prev_ref, y_ref):
    L, G, GW, NB = RWKV_CHUNK, RWKV_GROUP, RWKV_GROUP * HEAD_DIM, RWKV_BLOCK

    @pl.when(pl.program_id(1) == 0)
    def _():
        s_ref[...] = jnp.zeros_like(s_ref)
        prev_ref[...] = jnp.zeros_like(prev_ref)

    z = z_ref[...].astype(F32)
    row = lax.broadcasted_iota(jnp.int32, z.shape, 0)
    zp = jnp.where(row == 0, prev_ref[...], pltpu.roll(z, 1, 0))
    prev_ref[...] = z[NB - 1:NB, :]
    zs = z + mu_ref[...] * (zp - z)
    r = zs[:, 0:512]
    k = zs[:, 512:1024]
    v = zs[:, 1024:1536]
    xw = zs[:, 1536:1600]
    xa = zs[:, 1600:1664]
    gate = z[:, 1664:2176]

    t = -(w0_ref[...] + _bdot(jnp.tanh(xw), w2_ref[...]))
    softplus = jnp.maximum(t, 0.0) + jnp.log(1.0 + jnp.exp(-jnp.abs(t)))
    ld = -jnp.exp(-softplus - 0.5)
    a = _sigmoid(a0_ref[...] + _bdot(xa, a2_ref[...]))
    kk = k * kk_ref[...]
    k2 = k * (1.0 + (a - 1.0) * ka_ref[...])
    kk = kk * lax.rsqrt(_bdot(kk * kk, hsum_ref[...]) + 1e-12)
    b = kk * a

    cum = _dot_split(tri_ref[...], ld)
    tot = _dot_split(ones_ref[...], ld)
    e_neg = jnp.exp(-cum)
    e_rem = jnp.exp(tot - cum)
    rt = r * jnp.exp(cum)
    at = -kk * jnp.exp(cum - ld)
    kt = k2 * e_neg
    bt = b * e_neg
    kh = k2 * e_rem
    bh = b * e_rem
    w_tot = jnp.exp(tot)

    t_idx = lax.broadcasted_iota(jnp.int32, (L, GW), 0)
    s_idx = lax.broadcasted_iota(jnp.int32, (L, GW), 1) % L
    strict = s_idx < t_idx
    incl = s_idx <= t_idx
    eye = jnp.where(s_idx == t_idx, 1.0, 0.0)
    own = own_ref[...] != 0
    bd = bd_ref[...] != 0.0
    blk = lambda m: _block_diag(m, own)

    ngroups, nchunks = D_RWKV // GW, NB // L
    inst = [(slice(c * L, (c + 1) * L), slice(g * GW, (g + 1) * GW))
            for c in range(nchunks) for g in range(ngroups)]
    ars = [jnp.concatenate([at[i], rt[i]], axis=0) for i in inst]
    g_bs = [_bdot_nt(ar, blk(bt[i])) for ar, i in zip(ars, inst)]
    g_ks = [_bdot_nt(ar, blk(kt[i])) for ar, i in zip(ars, inst)]
    m_abs = [jnp.where(strict, m[0:L], 0.0) for m in g_bs]
    m_rbs = [jnp.where(incl, m[L:2 * L], 0.0) for m in g_bs]
    m_aks = [jnp.where(strict, m[0:L], 0.0) for m in g_ks]
    m_rks = [jnp.where(incl, m[L:2 * L], 0.0) for m in g_ks]
    ps = m_abs
    tinvs = [eye + p for p in ps]
    for _ in range(5):
        ps = [_bdot(p, blk(p)) for p in ps]
        tinvs = [t + _bdot(p, blk(t)) for p, t in zip(ps, tinvs)]
    v_bds = [blk(v[i]) for i in inst]
    tas = [_bdot(t, blk(at[i])) for t, i in zip(tinvs, inst)]
    mvs = [_bdot(m, vb) for m, vb in zip(m_aks, v_bds)]
    tmvs = [_bdot(t, blk(mv)) for t, mv in zip(tinvs, mvs)]
    ras = [rt[i] + _bdot(m, blk(ta)) for i, m, ta in zip(inst, m_rbs, tas)]
    y0s = [_bdot(mb, blk(tmv)) + _bdot(mk, vb)
           for mb, tmv, mk, vb in zip(m_rbs, tmvs, m_rks, v_bds)]
    qs = [jnp.where(bd, _bdot_tn(ta, bh[i]), 0.0) for ta, i in zip(tas, inst)]
    cmats = [jnp.where(bd, _bdot_tn(jnp.concatenate([tmv, v[i]], axis=0),
                                    jnp.concatenate([bh[i], kh[i]], axis=0)), 0.0)
             for tmv, i in zip(tmvs, inst)]
    states = [s_ref[g] for g in range(ngroups)]
    for j, (rs, sl) in enumerate(inst):
        g = j % ngroups
        s = states[g]
        y_ref[rs, sl] = _bdot_nt(ras[j], s) + y0s[j]
        states[g] = s * w_tot[rs.start:rs.start + 1, sl] + _bdot(s, qs[j]) + cmats[j]
    for g in range(ngroups):
        s_ref[g] = states[g]
    y = y_ref[...]

    hmean = hsum_ref[...] * (1.0 / HEAD_DIM)
    mean = _bdot(y, hmean)
    d = y - mean
    var = _bdot(d * d, hmean)
    yn = d * lax.rsqrt(var + RWKV_LN_EPS) * lnw_ref[...] + lnb_ref[...]
    bonus = _bdot(r * k2 * rk_ref[...], hsum_ref[...]) * v
    o_ref[...] = ((yn + bonus) * _silu(gate)).astype(BF16)


def _rwkv(za, batch, seq, p):
    L, gw, nb = RWKV_CHUNK, RWKV_GROUP * HEAD_DIM, RWKV_BLOCK
    idx = jnp.arange(RWKV_GROUP * L)
    own = ((idx[:, None] // L) == (jnp.arange(gw)[None, :] // HEAD_DIM)).astype(BF16)
    gid = jnp.arange(gw) // HEAD_DIM
    bd = (gid[:, None] == gid[None, :]).astype(F32)
    bidx = jnp.arange(nb)
    chunk_same = (bidx[:, None] // L) == (bidx[None, :] // L)
    tri = (chunk_same & (bidx[None, :] <= bidx[:, None])).astype(BF16)
    ones = chunk_same.astype(BF16)
    hid = jnp.arange(D_RWKV) // HEAD_DIM
    hsum = (hid[:, None] == hid[None, :]).astype(BF16)
    consts = (tri, ones, hsum, own, bd)
    full = lambda a: pl.BlockSpec(a.shape, lambda b, c: (0,) * a.ndim)
    nblk = seq // nb
    return pl.pallas_call(
        _rwkv_kernel,
        out_shape=jax.ShapeDtypeStruct((batch * seq, D_RWKV), BF16),
        grid=(batch, nblk),
        in_specs=[pl.BlockSpec((nb, N_A), lambda b, c: (b * nblk + c, 0))]
                 + [full(a) for a in p] + [full(a) for a in consts],
        out_specs=pl.BlockSpec((nb, D_RWKV), lambda b, c: (b * nblk + c, 0)),
        scratch_shapes=[pltpu.VMEM((D_RWKV // gw, gw, gw), F32),
                        pltpu.VMEM((1, N_A), F32),
                        pltpu.VMEM((nb, D_RWKV), F32)],
        compiler_params=pltpu.CompilerParams(dimension_semantics=("arbitrary", "arbitrary"),
                                             vmem_limit_bytes=VMEM_LIMIT),
        name="rwkv7",
    )(za, *p, *consts)


def _ret_kernel(z_ref, cos_ref, sin_ref, qw_ref, kw_ref, cd_ref, dmask_ref, hmean_ref, own_ref, bd_ref,
                o_ref, s_ref):
    C, G, GW = RET_CHUNK, RET_GROUP, RET_GROUP * HEAD_DIM

    @pl.when(pl.program_id(1) == 0)
    def _():
        s_ref[...] = jnp.zeros_like(s_ref)

    z = z_ref[...].astype(F32)
    lane = lax.broadcasted_iota(jnp.int32, (C, D_RET), 1)
    first_half = (lane % HEAD_DIM) < (HEAD_DIM // 2)
    cos = cos_ref[...]
    sin = sin_ref[...]

    def rope(x):
        swapped = jnp.where(first_half, pltpu.roll(x, D_RET - HEAD_DIM // 2, 1),
                            pltpu.roll(x, HEAD_DIM // 2, 1))
        return x * cos + swapped * sin

    q = rope(z[:, 0:512])
    k = rope(z[:, 512:1024]) * (HEAD_DIM ** -0.5)
    v = z[:, 1024:1536]
    gate = z[:, 1536:2048]
    qw = q * qw_ref[...]
    kw = k * kw_ref[...]
    cd = cd_ref[...]

    own = own_ref[...] != 0
    bd = bd_ref[...] != 0.0
    groups = [slice(g * GW, (g + 1) * GW) for g in range(D_RET // GW)]
    s0s = [s_ref[g] for g in range(len(groups))]
    scores = [_bdot_nt(q[:, sl], _block_diag(k[:, sl], own)) * dmask_ref[g]
              for g, sl in enumerate(groups)]
    cross = [_bdot(qw[:, sl], s0) for sl, s0 in zip(groups, s0s)]
    upd = [_bdot_tn(kw[:, sl], v[:, sl]) for sl in groups]
    intra = [_bdot(sc, _block_diag(v[:, sl], own)) for sc, sl in zip(scores, groups)]
    for g, sl in enumerate(groups):
        s_ref[g] = s0s[g] * cd[:, sl] + jnp.where(bd, upd[g], 0.0)
    y = jnp.concatenate([a + b for a, b in zip(intra, cross)], axis=1)

    hmean = hmean_ref[...]
    d = y - _bdot(y, hmean)
    var = _bdot(d * d, hmean)
    o_ref[...] = (d * lax.rsqrt(var + EPS) * _silu(gate)).astype(BF16)


def _retention(zb, batch, seq):
    C, gw = RET_CHUNK, RET_GROUP * HEAD_DIM
    half = HEAD_DIM // 2
    inv = ROPE_BASE ** (-jnp.arange(half, dtype=F32) / half)
    ang = jnp.arange(seq, dtype=F32)[:, None] * inv[None, :]
    cos = jnp.tile(jnp.cos(ang), (1, 2 * RET_HEADS))
    sin = jnp.tile(jnp.concatenate([-jnp.sin(ang), jnp.sin(ang)], axis=1), (1, RET_HEADS))
    log_gamma = jnp.log(1.0 - 2.0 ** (-5.0 - jnp.arange(RET_HEADS, dtype=F32)))
    idx = jnp.arange(C, dtype=F32)
    diff = idx[:, None] - idx[None, :]
    dmask = jnp.where(diff >= 0, jnp.exp(log_gamma[:, None, None] * jnp.maximum(diff, 0.0)), 0.0)
    ngr = RET_HEADS // RET_GROUP
    dm = dmask.reshape(ngr, RET_GROUP, C, C).transpose(0, 2, 1, 3).reshape(ngr, C, RET_GROUP * C)
    own = ((jnp.arange(RET_GROUP * C)[:, None] // C)
           == (jnp.arange(gw)[None, :] // HEAD_DIM)).astype(BF16)
    per_lane = lambda t: jnp.repeat(t, HEAD_DIM, axis=-1)
    qw = per_lane(jnp.exp(log_gamma[None, :] * (idx + 1.0)[:, None]))
    kw = per_lane(jnp.exp(log_gamma[None, :] * (C - 1.0 - idx)[:, None]))
    cd = per_lane(jnp.exp(log_gamma * C)[None, :])
    hid = jnp.arange(D_RET) // HEAD_DIM
    hmean = ((hid[:, None] == hid[None, :]).astype(F32) / HEAD_DIM).astype(BF16)
    gid = jnp.arange(gw) // HEAD_DIM
    bd = (gid[:, None] == gid[None, :]).astype(F32)
    nc = seq // C
    full = lambda a: pl.BlockSpec(a.shape, lambda b, c: (0,) * a.ndim)
    return pl.pallas_call(
        _ret_kernel,
        out_shape=jax.ShapeDtypeStruct((batch * seq, D_RET), BF16),
        grid=(batch, nc),
        in_specs=[pl.BlockSpec((C, N_B), lambda b, c: (b * nc + c, 0)),
                  pl.BlockSpec((C, D_RET), lambda b, c: (c, 0)),
                  pl.BlockSpec((C, D_RET), lambda b, c: (c, 0)),
                  full(qw), full(kw), full(cd), full(dm), full(hmean), full(own), full(bd)],
        out_specs=pl.BlockSpec((C, D_RET), lambda b, c: (b * nc + c, 0)),
        scratch_shapes=[pltpu.VMEM((D_RET // gw, gw, gw), F32)],
        compiler_params=pltpu.CompilerParams(dimension_semantics=("arbitrary", "arbitrary"),
                                             vmem_limit_bytes=VMEM_LIMIT),
        name="retention",
    )(zb, cos, sin, qw, kw, cd, dm, hmean, own, bd)


def _s5_kernel(u_ref, tz_ref, so_ref, si_ref, lam_ref, o_ref, uf_ref, dt_ref, yt_ref, *, nc):
    lc, n = S5_CHUNK, S5_STATE
    ncb = dt_ref.shape[-1]
    g = pl.program_id(1)

    ntile = uf_ref.shape[0]
    gpt = LANES // S5_GROUP

    @pl.when(g == 0)
    def _():
        for j in range(ntile):
            for c in range(ncb):
                uf_ref[j, c * S5_PITCH:c * S5_PITCH + lc, :] = (
                    u_ref[c * lc:(c + 1) * lc, j * LANES:(j + 1) * LANES].astype(F32))
        for s in range(lc):
            for j in range(ntile):
                tok = uf_ref[j, pl.ds(s, ncb, stride=S5_PITCH), :]
                dt_ref[j * gpt:(j + 1) * gpt, s] = tok.T.astype(BF16).reshape(gpt, S5_GROUP, ncb)

    gs = range(tz_ref.shape[0])
    ds = [dt_ref[g * len(gs) + i].reshape(lc * S5_GROUP, ncb) for i in gs]
    ys = [jnp.dot(tz_ref[i], ds[i], preferred_element_type=F32) for i in gs]
    incs = [jnp.dot(so_ref[i], ds[i], preferred_element_type=F32) for i in gs]
    zs = [(inc[0:n], inc[n:2 * n]) for inc in incs]
    chunk = lax.broadcasted_iota(jnp.int32, (n, ncb), 1) % nc
    shifted = lambda z, sh: jnp.where(chunk >= sh, pltpu.roll(z, sh, 1), 0.0)
    for lvl in range(lam_ref.shape[1]):
        nxt = []
        for i, (zr, zi) in enumerate(zs):
            ar, ai = lam_ref[i, lvl, 0], lam_ref[i, lvl, 1]
            pr, pi = shifted(zr, 1 << lvl), shifted(zi, 1 << lvl)
            nxt.append((zr + ar * pr - ai * pi, zi + ar * pi + ai * pr))
        zs = nxt
    for i, (zr, zi) in enumerate(zs):
        x0 = jnp.concatenate([shifted(zr, 1), shifted(zi, 1)], axis=0)
        y = ys[i] + jnp.dot(si_ref[i], x0.astype(BF16), preferred_element_type=F32)
        yt_ref[g * len(gs) + i] = y.reshape(lc, S5_GROUP, ncb)

    @pl.when(g == pl.num_programs(1) - 1)
    def _():
        for t in range(lc):
            for j in range(ntile):
                yt = yt_ref[j * gpt:(j + 1) * gpt, t].reshape(LANES, ncb)
                uf_ref[j, pl.ds(t, ncb, stride=S5_PITCH), :] = yt.T
        for j in range(ntile):
            for c in range(ncb):
                o_ref[c * lc:(c + 1) * lc, j * LANES:(j + 1) * LANES] = (
                    uf_ref[j, c * S5_PITCH:c * S5_PITCH + lc, :].astype(BF16))


def _s5_tables(a_re, a_im, log_dt, b_re, b_im, c_re, c_im, levels, lanes):
    lc, n, pch = S5_CHUNK, S5_STATE, S5_GROUP
    dt = jnp.exp(log_dt)[:, None]
    ab_re = jnp.exp(dt * a_re) * jnp.cos(dt * a_im)
    ab_im = jnp.exp(dt * a_re) * jnp.sin(dt * a_im)
    pr, qi = ab_re - 1.0, ab_im
    den = a_re * a_re + a_im * a_im
    cr = (pr * a_re + qi * a_im) / den
    ci = (qi * a_re - pr * a_im) / den
    bb_re = cr[..., None] * b_re - ci[..., None] * b_im
    bb_im = cr[..., None] * b_im + ci[..., None] * b_re
    j = jnp.arange(lc + 1, dtype=F32)[:, None, None]
    pw_mag = jnp.exp(j * (dt * a_re)[None])
    pw_re = pw_mag * jnp.cos(j * (dt * a_im)[None])
    pw_im = pw_mag * jnp.sin(j * (dt * a_im)[None])
    cl_re = c_re[None] * pw_re[:, :, None, :] - c_im[None] * pw_im[:, :, None, :]
    cl_im = c_re[None] * pw_im[:, :, None, :] + c_im[None] * pw_re[:, :, None, :]
    kern = (jnp.einsum('jgpn,gnq->jgpq', cl_re, bb_re)
            - jnp.einsum('jgpn,gnq->jgpq', cl_im, bb_im))
    s_idx = jnp.arange(lc)[:, None]
    t_idx = jnp.arange(lc)[None, :]
    lag = t_idx - s_idx
    tz = jnp.where((lag >= 0)[:, :, None, None, None], kern[jnp.maximum(lag, 0)], 0.0)
    tz = tz.transpose(2, 1, 3, 0, 4).reshape(S5_GROUPS, lc * pch, lc * pch)
    rev_re = pw_re[lc - 1::-1][:lc]
    rev_im = pw_im[lc - 1::-1][:lc]
    so_re = rev_re[..., None] * bb_re[None] - rev_im[..., None] * bb_im[None]
    so_im = rev_re[..., None] * bb_im[None] + rev_im[..., None] * bb_re[None]
    so = jnp.concatenate([so_re, so_im], axis=2)
    so = so.transpose(1, 2, 0, 3).reshape(S5_GROUPS, 2 * n, lc * pch)
    si = jnp.concatenate([cl_re[1:], -cl_im[1:]], axis=3)
    si = si.transpose(1, 0, 2, 3).reshape(S5_GROUPS, lc * pch, 2 * n)
    e = (lc * 2.0 ** jnp.arange(levels, dtype=F32))[None, :, None]
    mag = jnp.exp(e * (dt * a_re)[:, None, :])
    lam = jnp.stack([mag * jnp.cos(e * (dt * a_im)[:, None, :]),
                     mag * jnp.sin(e * (dt * a_im)[:, None, :])], axis=2)
    lam = jnp.broadcast_to(lam[..., None], lam.shape + (lanes,))
    return tz.astype(BF16), so.astype(BF16), si.astype(BF16), lam


def _s5(zc, batch, seq, params):
    lc = S5_CHUNK
    nc = seq // lc
    nb = S5_SEQS if batch % S5_SEQS == 0 else 1
    ncb = nb * nc
    levels = max(1, (nc - 1).bit_length())
    tz, so, si, lam = _s5_tables(*params, levels, ncb)
    per_group = lambda a: pl.BlockSpec((S5_GROUPS_PER_STEP,) + a.shape[1:],
                                       lambda i, g: (g,) + (0,) * (a.ndim - 1))
    tokens = pl.BlockSpec((nb * seq, D_S5), lambda i, g: (i, 0))
    return pl.pallas_call(
        functools.partial(_s5_kernel, nc=nc),
        out_shape=jax.ShapeDtypeStruct((batch * seq, D_S5), BF16),
        grid=(batch // nb, S5_GROUPS // S5_GROUPS_PER_STEP),
        in_specs=[tokens, per_group(tz), per_group(so), per_group(si), per_group(lam)],
        out_specs=tokens,
        scratch_shapes=[pltpu.VMEM((D_S5 // LANES, ncb * S5_PITCH, LANES), F32),
                        pltpu.VMEM((S5_GROUPS, lc, S5_GROUP, ncb), BF16),
                        pltpu.VMEM((S5_GROUPS, lc, S5_GROUP, ncb), F32)],
        compiler_params=pltpu.CompilerParams(dimension_semantics=("arbitrary", "arbitrary"),
                                             vmem_limit_bytes=VMEM_LIMIT),
        name="s5",
    )(zc, tz, so, si, lam)


def _merge_kernel(ya_ref, yb_ref, ys_ref, zc_ref, zg_ref, x_ref, dskip_ref, gluw_ref, glub_ref,
                  wa_ref, wb_ref, wc_ref, bm_ref, wo_ref, pn_ref, o_ref):
    zc = zc_ref[...].astype(F32)
    y = ys_ref[...].astype(F32) + dskip_ref[...] * zc[:, 0:D_S5]
    gz = 0.5 * y * (1.0 + jnp.tanh(math.sqrt(2.0 / math.pi) * (y + 0.044715 * (y * y * y))))
    yc = gz * _sigmoid(_bdot(gz, gluw_ref[...]) + glub_ref[...]) * _silu(zc[:, D_S5:2 * D_S5])
    zg = zg_ref[...].astype(F32)
    bm = bm_ref[...]
    merged = (_sigmoid(zg[:, 0:D_MODEL] + bm[0:1, :])
              * jnp.dot(ya_ref[...], wa_ref[...], preferred_element_type=F32)
              + _sigmoid(zg[:, D_MODEL:2 * D_MODEL] + bm[1:2, :])
              * jnp.dot(yb_ref[...], wb_ref[...], preferred_element_type=F32)
              + _sigmoid(zg[:, 2 * D_MODEL:3 * D_MODEL] + bm[2:3, :]) * _bdot(yc, wc_ref[...]))
    out = _bdot(merged, wo_ref[...])
    normed = out * lax.rsqrt(jnp.mean(out * out, axis=-1, keepdims=True) + EPS) * pn_ref[...]
    o_ref[...] = x_ref[...] + normed


def _merge(ya, yb, ys, zc, zg, x2, params, tm=256):
    m = x2.shape[0]
    tok = lambda n: pl.BlockSpec((tm, n), lambda i: (i, 0))
    full = lambda a: pl.BlockSpec(a.shape, lambda i: (0,) * a.ndim)
    return pl.pallas_call(
        _merge_kernel,
        out_shape=jax.ShapeDtypeStruct((m, D_MODEL), F32),
        grid=(m // tm,),
        in_specs=[tok(D_RWKV), tok(D_RET), tok(D_S5), tok(N_C), tok(N_G), tok(D_MODEL)]
                 + [full(a) for a in params],
        out_specs=tok(D_MODEL),
        compiler_params=pltpu.CompilerParams(dimension_semantics=("arbitrary",),
                                             vmem_limit_bytes=VMEM_LIMIT),
        name="merge_out",
    )(ya, yb, ys, zc, zg, x2, *params)


def kernel(x, pre_norm, w_in, rwkv_mu_rkv, rwkv_mu_wa, rwkv_w0, rwkv_w2, rwkv_a0, rwkv_a2,
           rwkv_k_k, rwkv_k_a, rwkv_r_k, rwkv_ln_w, rwkv_ln_b, s5_A_re, s5_A_im, s5_log_dt,
           s5_B_re, s5_B_im, s5_C_re, s5_C_im, s5_D, s5_glu_w, s5_glu_b, w_proj_rwkv,
           w_proj_ret, w_proj_s5, b_merge, w_out, post_norm):
    batch, seq, _ = x.shape
    x2 = x.reshape(batch * seq, D_MODEL)
    row = lambda a: a.reshape(1, -1).astype(F32)
    for l in range(DEPTH):
        za, zb, zc, zg = _in_proj(x2, row(pre_norm[l]), w_in[l].astype(BF16))
        mu = jnp.concatenate([rwkv_mu_rkv[l].reshape(-1), rwkv_mu_wa[l].reshape(-1),
                              jnp.zeros((D_RWKV,), F32)]).reshape(1, N_A)
        rwkv_params = (mu, row(rwkv_w0[l]), rwkv_w2[l].astype(BF16), row(rwkv_a0[l]),
                       rwkv_a2[l].astype(BF16), row(rwkv_k_k[l]), row(rwkv_k_a[l]),
                       row(rwkv_r_k[l]), row(rwkv_ln_w[l]), row(rwkv_ln_b[l]))
        ya = _rwkv(za, batch, seq, rwkv_params)
        yb = _retention(zb, batch, seq)
        ys = _s5(zc, batch, seq, (s5_A_re[l], s5_A_im[l], s5_log_dt[l], s5_B_re[l], s5_B_im[l],
                                  s5_C_re[l], s5_C_im[l]))
        merge_params = (row(s5_D[l]), s5_glu_w[l].astype(BF16), row(s5_glu_b[l]),
                        w_proj_rwkv[l].astype(BF16), w_proj_ret[l].astype(BF16),
                        w_proj_s5[l].astype(BF16), b_merge[l].astype(F32),
                        w_out[l].astype(BF16), row(post_norm[l]))
        x2 = _merge(ya, yb, ys, zc, zg, x2, merge_params)
    return x2.reshape(batch, seq, D_MODEL)
```

```python
import functools
import math

import jax
import jax.numpy as jnp
from jax import lax
from jax.experimental import pallas as pl
from jax.experimental.pallas import tpu as pltpu

F32 = jnp.float32
BF16 = jnp.bfloat16

D_MODEL = 1024
DEPTH = 2
HEAD_DIM = 64
D_RWKV = 512
W_LORA = 64
A_LORA = 64
RWKV_LN_EPS = 64e-5
D_RET = 512
RET_HEADS = 8
RET_CHUNK = 128
ROPE_BASE = 10000.0
S5_GROUP = 16
S5_STATE = 64
D_S5 = 512
S5_GROUPS = D_S5 // S5_GROUP
EPS = 1e-6

N_A = 4 * D_RWKV + W_LORA + A_LORA
N_B = 4 * D_RET
N_C = 2 * D_S5
N_G = 3 * D_MODEL
D_IN = N_A + N_B + N_C + N_G

RWKV_CHUNK = 64
RWKV_BLOCK = 512
RWKV_GROUP = 4
RET_GROUP = 2
RET_BLOCK = 256
S5_CHUNK = 32
S5_SEQS = 2
S5_GROUPS_PER_STEP = 4
S5_PITCH = 40

VMEM_LIMIT = 56 * 1024 * 1024
LANES = 128


def _bdot(a, b):
    return jnp.dot(a.astype(BF16), b.astype(BF16), preferred_element_type=F32)


def _bdot_nt(a, b):
    return lax.dot_general(a.astype(BF16), b.astype(BF16), (((1,), (1,)), ((), ())),
                           preferred_element_type=F32)


def _bdot_tn(a, b):
    return lax.dot_general(a.astype(BF16), b.astype(BF16), (((0,), (0,)), ((), ())),
                           preferred_element_type=F32)


def _dot_split(m, x):
    hi = x.astype(BF16)
    lo = (x - hi.astype(F32)).astype(BF16)
    return (jnp.dot(m, hi, preferred_element_type=F32)
            + jnp.dot(m, lo, preferred_element_type=F32))


def _sigmoid(x):
    return 0.5 * jnp.tanh(0.5 * x) + 0.5


def _silu(x):
    return x * _sigmoid(x)


def _head_lane_masks(nheads):
    lane_head = jnp.arange(nheads * HEAD_DIM)[None, :] // HEAD_DIM
    return (lane_head == jnp.arange(nheads)[:, None]).astype(BF16)


def _head_sums(x, ones_bd):
    w = ones_bd.shape[0]
    return jnp.concatenate([_bdot(x[:, i:i + w], ones_bd) for i in range(0, x.shape[1], w)], axis=1)


def _block_diag(x, own):
    x = x.astype(BF16)
    return jnp.concatenate([x * own[h:h + 1, :] for h in range(own.shape[0])], axis=0)


def _in_proj_kernel(x_ref, g_ref, w_ref, oa_ref, ob_ref, oc_ref, og_ref):
    x = x_ref[...]
    h = (x * lax.rsqrt(jnp.mean(x * x, axis=-1, keepdims=True) + EPS) * g_ref[...]).astype(BF16)
    col = 0
    for o_ref in (oa_ref, ob_ref, oc_ref, og_ref):
        n = o_ref.shape[-1]
        for c0 in range(0, n, 512):
            c1 = min(c0 + 512, n)
            o_ref[:, c0:c1] = jnp.dot(h, w_ref[:, col + c0:col + c1],
                                      preferred_element_type=F32).astype(BF16)
        col += n


def _in_proj(x2, g, w, layer, tm=512):
    m = x2.shape[0]
    tm = min(tm, m)
    widths = (N_A, N_B, N_C, N_G)
    return pl.pallas_call(
        _in_proj_kernel,
        out_shape=[jax.ShapeDtypeStruct((m, n), BF16) for n in widths],
        grid=(m // tm,),
        in_specs=[pl.BlockSpec((tm, D_MODEL), lambda i: (i, 0)),
                  pl.BlockSpec((1, D_MODEL), lambda i: (0, 0)),
                  pl.BlockSpec((None, D_MODEL, D_IN), lambda i: (layer, 0, 0),
                               pipeline_mode=pl.Buffered(1))],
        out_specs=[pl.BlockSpec((tm, n), lambda i: (i, 0)) for n in widths],
        compiler_params=pltpu.CompilerParams(dimension_semantics=("arbitrary",),
                                             vmem_limit_bytes=VMEM_LIMIT),
        name="in_proj",
    )(x2, g, w)


def _rwkv_kernel(z_ref, mu_ref, w0_ref, w2_ref, a0_ref, a2_ref, kk_ref, ka_ref, rk_ref,
                 lnw_ref, lnb_ref, tri_ref, hsum_ref, own_ref, bd_ref,
                 o_ref, s_ref, prev_ref, y_ref):
    L, G, GW, NB = RWKV_CHUNK, RWKV_GROUP, RWKV_GROUP * HEAD_DIM, z_ref.shape[0]

    @pl.when(pl.program_id(1) == 0)
    def _():
        s_ref[...] = jnp.zeros_like(s_ref)
        prev_ref[...] = jnp.zeros_like(prev_ref)

    z = z_ref[...].astype(F32)
    row = lax.broadcasted_iota(jnp.int32, z.shape, 0)
    zp = jnp.where(row == 0, prev_ref[...], pltpu.roll(z, 1, 0))
    prev_ref[...] = z[NB - 1:NB, :]
    zs = z + mu_ref[...] * (zp - z)
    r = zs[:, 0:512]
    k = zs[:, 512:1024]
    v = zs[:, 1024:1536]
    xw = zs[:, 1536:1600]
    xa = zs[:, 1600:1664]
    gate = z[:, 1664:2176]

    t = -(w0_ref[...] + _bdot(jnp.tanh(xw), w2_ref[...]))
    softplus = jnp.maximum(t, 0.0) + jnp.log(1.0 + jnp.exp(-jnp.abs(t)))
    ld = -jnp.exp(-softplus - 0.5)
    a = _sigmoid(a0_ref[...] + _bdot(xa, a2_ref[...]))
    kk = k * kk_ref[...]
    k2 = k * (1.0 + (a - 1.0) * ka_ref[...])
    hsum = hsum_ref[...]
    kk = kk * lax.rsqrt(_head_sums(kk * kk, hsum) + 1e-12)
    b = kk * a

    tri = tri_ref[...]
    tb = tri.shape[0]
    cum = jnp.concatenate([_dot_split(tri, ld[i:i + tb]) for i in range(0, NB, tb)],
                          axis=0)
    tot = jnp.concatenate([jnp.broadcast_to(cum[c * L + L - 1:(c + 1) * L], (L, D_RWKV))
                           for c in range(NB // L)], axis=0)
    e_neg = jnp.exp(-cum)
    e_rem = jnp.exp(tot - cum)
    rt = r * jnp.exp(cum)
    at = -kk * jnp.exp(cum - ld)
    kt = k2 * e_neg
    bt = b * e_neg
    kh = k2 * e_rem
    bh = b * e_rem
    w_tot = jnp.exp(tot)

    t_idx = lax.broadcasted_iota(jnp.int32, (L, GW), 0)
    s_idx = lax.broadcasted_iota(jnp.int32, (L, GW), 1) % L
    strict = s_idx < t_idx
    incl = s_idx <= t_idx
    eye = jnp.where(s_idx == t_idx, 1.0, 0.0)
    own = own_ref[...]
    bd = bd_ref[...] != 0.0
    blk = lambda m: _block_diag(m, own)

    ngroups, nchunks = D_RWKV // GW, NB // L
    inst = [(slice(c * L, (c + 1) * L), slice(g * GW, (g + 1) * GW))
            for c in range(nchunks) for g in range(ngroups)]
    ars = [jnp.concatenate([at[i], rt[i]], axis=0) for i in inst]
    g_bs = [_bdot_nt(ar, blk(bt[i])) for ar, i in zip(ars, inst)]
    g_ks = [_bdot_nt(ar, blk(kt[i])) for ar, i in zip(ars, inst)]
    m_abs = [jnp.where(strict, m[0:L], 0.0) for m in g_bs]
    m_rbs = [jnp.where(incl, m[L:2 * L], 0.0) for m in g_bs]
    m_aks = [jnp.where(strict, m[0:L], 0.0) for m in g_ks]
    m_rks = [jnp.where(incl, m[L:2 * L], 0.0) for m in g_ks]
    ps = m_abs
    tinvs = [eye + p for p in ps]
    for _ in range(5):
        ps = [_bdot(p, blk(p)) for p in ps]
        tinvs = [t + _bdot(p, blk(t)) for p, t in zip(ps, tinvs)]
    v_bds = [blk(v[i]) for i in inst]
    tas = [_bdot(t, blk(at[i])) for t, i in zip(tinvs, inst)]
    mvs = [_bdot(m, vb) for m, vb in zip(m_aks, v_bds)]
    tmvs = [_bdot(t, blk(mv)) for t, mv in zip(tinvs, mvs)]
    ras = [rt[i] + _bdot(m, blk(ta)) for i, m, ta in zip(inst, m_rbs, tas)]
    y0s = [_bdot(mb, blk(tmv)) + _bdot(mk, vb)
           for mb, tmv, mk, vb in zip(m_rbs, tmvs, m_rks, v_bds)]
    qs = [jnp.where(bd, _bdot_tn(ta, bh[i]), 0.0) for ta, i in zip(tas, inst)]
    cmats = [jnp.where(bd, _bdot_tn(jnp.concatenate([tmv, v[i]], axis=0),
                                    jnp.concatenate([bh[i], kh[i]], axis=0)), 0.0)
             for tmv, i in zip(tmvs, inst)]
    states = [s_ref[g] for g in range(ngroups)]
    for j, (rs, sl) in enumerate(inst):
        g = j % ngroups
        s = states[g]
        y_ref[rs, sl] = _bdot_nt(ras[j], s) + y0s[j]
        states[g] = s * w_tot[rs.start:rs.start + 1, sl] + _bdot(s, qs[j]) + cmats[j]
    for g in range(ngroups):
        s_ref[g] = states[g]
    y = y_ref[...]

    d = y - _head_sums(y, hsum) * (1.0 / HEAD_DIM)
    var = _head_sums(d * d, hsum) * (1.0 / HEAD_DIM)
    yn = d * lax.rsqrt(var + RWKV_LN_EPS) * lnw_ref[...] + lnb_ref[...]
    bonus = _head_sums(r * k2 * rk_ref[...], hsum) * v
    o_ref[...] = ((yn + bonus) * _silu(gate)).astype(BF16)


def _rwkv(za, batch, seq, p):
    L, gw, nb = RWKV_CHUNK, RWKV_GROUP * HEAD_DIM, min(RWKV_BLOCK, seq)
    own = _head_lane_masks(RWKV_GROUP)
    gid = jnp.arange(gw) // HEAD_DIM
    bd = (gid[:, None] == gid[None, :]).astype(F32)
    bidx = jnp.arange(gw)
    chunk_same = (bidx[:, None] // L) == (bidx[None, :] // L)
    tri = (chunk_same & (bidx[None, :] <= bidx[:, None])).astype(BF16)
    hsum = bd.astype(BF16)
    consts = (tri, hsum, own, bd)
    full = lambda a: pl.BlockSpec(a.shape, lambda b, c: (0,) * a.ndim)
    nblk = seq // nb
    return pl.pallas_call(
        _rwkv_kernel,
        out_shape=jax.ShapeDtypeStruct((batch * seq, D_RWKV), BF16),
        grid=(batch, nblk),
        in_specs=[pl.BlockSpec((nb, N_A), lambda b, c: (b * nblk + c, 0))]
                 + [full(a) for a in p] + [full(a) for a in consts],
        out_specs=pl.BlockSpec((nb, D_RWKV), lambda b, c: (b * nblk + c, 0)),
        scratch_shapes=[pltpu.VMEM((D_RWKV // gw, gw, gw), F32),
                        pltpu.VMEM((1, N_A), F32),
                        pltpu.VMEM((nb, D_RWKV), F32)],
        compiler_params=pltpu.CompilerParams(dimension_semantics=("arbitrary", "arbitrary"),
                                             vmem_limit_bytes=VMEM_LIMIT),
        name="rwkv7",
    )(za, *p, *consts)


def _ret_kernel(z_ref, cos_ref, sin_ref, qw_ref, kw_ref, cd_ref, dmask_ref, own_ref, bd_ref,
                o_ref, s_ref):
    C, GW, NB = RET_CHUNK, RET_GROUP * HEAD_DIM, z_ref.shape[0]

    @pl.when(pl.program_id(1) == 0)
    def _():
        s_ref[...] = jnp.zeros_like(s_ref)

    z = z_ref[...].astype(F32)
    lane = lax.broadcasted_iota(jnp.int32, (NB, D_RET), 1)
    first_half = (lane % HEAD_DIM) < (HEAD_DIM // 2)
    cos = cos_ref[...]
    sin = sin_ref[...]

    def rope(x):
        swapped = jnp.where(first_half, pltpu.roll(x, D_RET - HEAD_DIM // 2, 1),
                            pltpu.roll(x, HEAD_DIM // 2, 1))
        return x * cos + swapped * sin

    q = rope(z[:, 0:512])
    k = rope(z[:, 512:1024]) * (HEAD_DIM ** -0.5)
    v = z[:, 1024:1536]
    gate = z[:, 1536:2048]
    cd = cd_ref[...]
    own = own_ref[...]
    bd = bd_ref[...] != 0.0

    ngroups = D_RET // GW
    inst = [(slice(c * C, (c + 1) * C), slice(g * GW, (g + 1) * GW), g)
            for c in range(NB // C) for g in range(ngroups)]
    scores = [_bdot_nt(q[rs, sl], _block_diag(k[rs, sl], own)) * dmask_ref[g]
              for rs, sl, g in inst]
    upd = [_bdot_tn(k[rs, sl] * kw_ref[:, sl], v[rs, sl]) for rs, sl, g in inst]
    intra = [_bdot(sc, _block_diag(v[rs, sl], own)) for sc, (rs, sl, g) in zip(scores, inst)]
    states = [s_ref[g] for g in range(ngroups)]
    ys = [[None] * ngroups for _ in range(NB // C)]
    for j, (rs, sl, g) in enumerate(inst):
        ys[j // ngroups][g] = intra[j] + _bdot(q[rs, sl] * qw_ref[:, sl], states[g])
        states[g] = states[g] * cd[:, sl] + jnp.where(bd, upd[j], 0.0)
    for g in range(ngroups):
        s_ref[g] = states[g]
    y = jnp.concatenate([jnp.concatenate(row, axis=1) for row in ys], axis=0)

    hsum = bd_ref[...].astype(BF16)
    d = y - _head_sums(y, hsum) * (1.0 / HEAD_DIM)
    var = _head_sums(d * d, hsum) * (1.0 / HEAD_DIM)
    o_ref[...] = (d * lax.rsqrt(var + EPS) * _silu(gate)).astype(BF16)


def _retention(zb, batch, seq):
    C, gw = RET_CHUNK, RET_GROUP * HEAD_DIM
    half = HEAD_DIM // 2
    inv = ROPE_BASE ** (-jnp.arange(half, dtype=F32) / half)
    ang = jnp.arange(seq, dtype=F32)[:, None] * inv[None, :]
    cos = jnp.tile(jnp.cos(ang), (1, 2 * RET_HEADS))
    sin = jnp.tile(jnp.concatenate([-jnp.sin(ang), jnp.sin(ang)], axis=1), (1, RET_HEADS))
    log_gamma = jnp.log(1.0 - 2.0 ** (-5.0 - jnp.arange(RET_HEADS, dtype=F32)))
    idx = jnp.arange(C, dtype=F32)
    diff = idx[:, None] - idx[None, :]
    dmask = jnp.where(diff >= 0, jnp.exp(log_gamma[:, None, None] * jnp.maximum(diff, 0.0)), 0.0)
    ngr = RET_HEADS // RET_GROUP
    dm = dmask.reshape(ngr, RET_GROUP, C, C).transpose(0, 2, 1, 3).reshape(ngr, C, RET_GROUP * C)
    own = _head_lane_masks(RET_GROUP)
    per_lane = lambda t: jnp.repeat(t, HEAD_DIM, axis=-1)
    qw = per_lane(jnp.exp(log_gamma[None, :] * (idx + 1.0)[:, None]))
    kw = per_lane(jnp.exp(log_gamma[None, :] * (C - 1.0 - idx)[:, None]))
    cd = per_lane(jnp.exp(log_gamma * C)[None, :])
    gid = jnp.arange(gw) // HEAD_DIM
    bd = (gid[:, None] == gid[None, :]).astype(F32)
    nb = min(RET_BLOCK, seq)
    nblk = seq // nb
    full = lambda a: pl.BlockSpec(a.shape, lambda b, c: (0,) * a.ndim)
    return pl.pallas_call(
        _ret_kernel,
        out_shape=jax.ShapeDtypeStruct((batch * seq, D_RET), BF16),
        grid=(batch, nblk),
        in_specs=[pl.BlockSpec((nb, N_B), lambda b, c: (b * nblk + c, 0)),
                  pl.BlockSpec((nb, D_RET), lambda b, c: (c, 0)),
                  pl.BlockSpec((nb, D_RET), lambda b, c: (c, 0)),
                  full(qw), full(kw), full(cd), full(dm), full(own), full(bd)],
        out_specs=pl.BlockSpec((nb, D_RET), lambda b, c: (b * nblk + c, 0)),
        scratch_shapes=[pltpu.VMEM((D_RET // gw, gw, gw), F32)],
        compiler_params=pltpu.CompilerParams(dimension_semantics=("arbitrary", "arbitrary"),
                                             vmem_limit_bytes=VMEM_LIMIT),
        name="retention",
    )(zb, cos, sin, qw, kw, cd, dm, own, bd)


def _s5_kernel(u_ref, tz_ref, so_ref, si_ref, lam_ref, o_ref, uf_ref, dt_ref, yt_ref, *, nc):
    lc, n = S5_CHUNK, S5_STATE
    ncb = dt_ref.shape[-1]
    g = pl.program_id(1)

    ntile = uf_ref.shape[0]
    gpt = LANES // S5_GROUP

    @pl.when(g == 0)
    def _():
        for j in range(ntile):
            for c in range(ncb):
                uf_ref[j, c * S5_PITCH:c * S5_PITCH + lc, :] = (
                    u_ref[c * lc:(c + 1) * lc, j * LANES:(j + 1) * LANES].astype(F32))
        for s in range(lc):
            for j in range(ntile):
                tok = uf_ref[j, pl.ds(s, ncb, stride=S5_PITCH), :]
                dt_ref[j * gpt:(j + 1) * gpt, s] = tok.T.astype(BF16).reshape(gpt, S5_GROUP, ncb)

    gs = range(tz_ref.shape[0])
    ds = [dt_ref[g * len(gs) + i].reshape(lc * S5_GROUP, ncb) for i in gs]
    ys = [jnp.dot(tz_ref[i], ds[i], preferred_element_type=F32) for i in gs]
    incs = [jnp.dot(so_ref[i], ds[i], preferred_element_type=F32) for i in gs]
    zs = [(inc[0:n], inc[n:2 * n]) for inc in incs]
    chunk = lax.broadcasted_iota(jnp.int32, (n, ncb), 1) % nc
    shifted = lambda z, sh: jnp.where(chunk >= sh, pltpu.roll(z, sh, 1), 0.0)
    for lvl in range(lam_ref.shape[1]):
        nxt = []
        for i, (zr, zi) in enumerate(zs):
            ar, ai = lam_ref[i, lvl, 0], lam_ref[i, lvl, 1]
            pr, pi = shifted(zr, 1 << lvl), shifted(zi, 1 << lvl)
            nxt.append((zr + ar * pr - ai * pi, zi + ar * pi + ai * pr))
        zs = nxt
    for i, (zr, zi) in enumerate(zs):
        x0 = jnp.concatenate([shifted(zr, 1), shifted(zi, 1)], axis=0)
        y = ys[i] + jnp.dot(si_ref[i], x0.astype(BF16), preferred_element_type=F32)
        yt_ref[g * len(gs) + i] = y.reshape(lc, S5_GROUP, ncb)

    @pl.when(g == pl.num_programs(1) - 1)
    def _():
        for t in range(lc):
            for j in range(ntile):
                yt = yt_ref[j * gpt:(j + 1) * gpt, t].reshape(LANES, ncb)
                uf_ref[j, pl.ds(t, ncb, stride=S5_PITCH), :] = yt.T
        for j in range(ntile):
            for c in range(ncb):
                o_ref[c * lc:(c + 1) * lc, j * LANES:(j + 1) * LANES] = (
                    uf_ref[j, c * S5_PITCH:c * S5_PITCH + lc, :].astype(BF16))


def _s5_tables(a_re, a_im, log_dt, b_re, b_im, c_re, c_im, levels, lanes):
    lc, n, pch = S5_CHUNK, S5_STATE, S5_GROUP
    dt = jnp.exp(log_dt)[:, None]
    ab_re = jnp.exp(dt * a_re) * jnp.cos(dt * a_im)
    ab_im = jnp.exp(dt * a_re) * jnp.sin(dt * a_im)
    pr, qi = ab_re - 1.0, ab_im
    den = a_re * a_re + a_im * a_im
    cr = (pr * a_re + qi * a_im) / den
    ci = (qi * a_re - pr * a_im) / den
    bb_re = cr[..., None] * b_re - ci[..., None] * b_im
    bb_im = cr[..., None] * b_im + ci[..., None] * b_re
    j = jnp.arange(lc + 1, dtype=F32)[:, None, None]
    pw_mag = jnp.exp(j * (dt * a_re)[None])
    pw_re = pw_mag * jnp.cos(j * (dt * a_im)[None])
    pw_im = pw_mag * jnp.sin(j * (dt * a_im)[None])
    cl_re = c_re[None] * pw_re[:, :, None, :] - c_im[None] * pw_im[:, :, None, :]
    cl_im = c_re[None] * pw_im[:, :, None, :] + c_im[None] * pw_re[:, :, None, :]
    kern = (jnp.einsum('jgpn,gnq->jgpq', cl_re, bb_re)
            - jnp.einsum('jgpn,gnq->jgpq', cl_im, bb_im))
    by_lag = kern[lc - 1::-1][:lc].transpose(1, 2, 0, 3).reshape(S5_GROUPS, pch, lc * pch)
    padded = jnp.pad(by_lag, ((0, 0), (0, 0), (0, (lc - 1) * pch)))
    tz = jnp.stack([padded[:, :, (lc - 1 - t) * pch:(2 * lc - 1 - t) * pch] for t in range(lc)],
                   axis=1).reshape(S5_GROUPS, lc * pch, lc * pch)
    rev_re = pw_re[lc - 1::-1][:lc]
    rev_im = pw_im[lc - 1::-1][:lc]
    so_re = rev_re[..., None] * bb_re[None] - rev_im[..., None] * bb_im[None]
    so_im = rev_re[..., None] * bb_im[None] + rev_im[..., None] * bb_re[None]
    so = jnp.concatenate([so_re, so_im], axis=2)
    so = so.transpose(1, 2, 0, 3).reshape(S5_GROUPS, 2 * n, lc * pch)
    si = jnp.concatenate([cl_re[1:], -cl_im[1:]], axis=3)
    si = si.transpose(1, 0, 2, 3).reshape(S5_GROUPS, lc * pch, 2 * n)
    e = (lc * 2.0 ** jnp.arange(levels, dtype=F32))[None, :, None]
    mag = jnp.exp(e * (dt * a_re)[:, None, :])
    lam = jnp.stack([mag * jnp.cos(e * (dt * a_im)[:, None, :]),
                     mag * jnp.sin(e * (dt * a_im)[:, None, :])], axis=2)
    lam = jnp.broadcast_to(lam[..., None], lam.shape + (lanes,))
    return tz.astype(BF16), so.astype(BF16), si.astype(BF16), lam


def _s5(zc, batch, seq, params):
    lc = S5_CHUNK
    nc = seq // lc
    nb = S5_SEQS if batch % S5_SEQS == 0 else 1
    ncb = nb * nc
    levels = max(1, (nc - 1).bit_length())
    tz, so, si, lam = _s5_tables(*params, levels, ncb)
    per_group = lambda a: pl.BlockSpec((S5_GROUPS_PER_STEP,) + a.shape[1:],
                                       lambda i, g: (g,) + (0,) * (a.ndim - 1))
    tokens = pl.BlockSpec((nb * seq, D_S5), lambda i, g: (i, 0))
    return pl.pallas_call(
        functools.partial(_s5_kernel, nc=nc),
        out_shape=jax.ShapeDtypeStruct((batch * seq, D_S5), BF16),
        grid=(batch // nb, S5_GROUPS // S5_GROUPS_PER_STEP),
        in_specs=[tokens, per_group(tz), per_group(so), per_group(si), per_group(lam)],
        out_specs=tokens,
        scratch_shapes=[pltpu.VMEM((D_S5 // LANES, ncb * S5_PITCH, LANES), F32),
                        pltpu.VMEM((S5_GROUPS, lc, S5_GROUP, ncb), BF16),
                        pltpu.VMEM((S5_GROUPS, lc, S5_GROUP, ncb), F32)],
        compiler_params=pltpu.CompilerParams(dimension_semantics=("arbitrary", "arbitrary"),
                                             vmem_limit_bytes=VMEM_LIMIT),
        name="s5",
    )(zc, tz, so, si, lam)


def _merge_kernel(ya_ref, yb_ref, ys_ref, zc_ref, zg_ref, x_ref, dskip_ref, gluw_ref, glub_ref,
                  wa_ref, wb_ref, wc_ref, bm_ref, wo_ref, pn_ref, o_ref):
    zc = zc_ref[...].astype(F32)
    y = ys_ref[...].astype(F32) + dskip_ref[...] * zc[:, 0:D_S5]
    gz = 0.5 * y * (1.0 + jnp.tanh(math.sqrt(2.0 / math.pi) * (y + 0.044715 * (y * y * y))))
    yc = gz * _sigmoid(_bdot(gz, gluw_ref[...]) + glub_ref[...]) * _silu(zc[:, D_S5:2 * D_S5])
    zg = zg_ref[...].astype(F32)
    bm = bm_ref[...]
    merged = (_sigmoid(zg[:, 0:D_MODEL] + bm[0:1, :])
              * jnp.dot(ya_ref[...], wa_ref[...], preferred_element_type=F32)
              + _sigmoid(zg[:, D_MODEL:2 * D_MODEL] + bm[1:2, :])
              * jnp.dot(yb_ref[...], wb_ref[...], preferred_element_type=F32)
              + _sigmoid(zg[:, 2 * D_MODEL:3 * D_MODEL] + bm[2:3, :]) * _bdot(yc, wc_ref[...]))
    out = _bdot(merged, wo_ref[...])
    normed = out * lax.rsqrt(jnp.mean(out * out, axis=-1, keepdims=True) + EPS) * pn_ref[...]
    o_ref[...] = x_ref[...] + normed


def _merge(ya, yb, ys, zc, zg, x2, params, tm=256):
    m = x2.shape[0]
    tok = lambda n: pl.BlockSpec((tm, n), lambda i: (i, 0))
    full = lambda a: pl.BlockSpec(a.shape, lambda i: (0,) * a.ndim)
    return pl.pallas_call(
        _merge_kernel,
        out_shape=jax.ShapeDtypeStruct((m, D_MODEL), F32),
        grid=(m // tm,),
        in_specs=[tok(D_RWKV), tok(D_RET), tok(D_S5), tok(N_C), tok(N_G), tok(D_MODEL)]
                 + [full(a) for a in params],
        out_specs=tok(D_MODEL),
        compiler_params=pltpu.CompilerParams(dimension_semantics=("arbitrary",),
                                             vmem_limit_bytes=VMEM_LIMIT),
        name="merge_out",
    )(ya, yb, ys, zc, zg, x2, *params)


def kernel(x, pre_norm, w_in, rwkv_mu_rkv, rwkv_mu_wa, rwkv_w0, rwkv_w2, rwkv_a0, rwkv_a2,
           rwkv_k_k, rwkv_k_a, rwkv_r_k, rwkv_ln_w, rwkv_ln_b, s5_A_re, s5_A_im, s5_log_dt,
           s5_B_re, s5_B_im, s5_C_re, s5_C_im, s5_D, s5_glu_w, s5_glu_b, w_proj_rwkv,
           w_proj_ret, w_proj_s5, b_merge, w_out, post_norm):
    batch, seq, _ = x.shape
    x2 = x.reshape(batch * seq, D_MODEL)
    row = lambda a: a.reshape(1, -1).astype(F32)
    w_in_bf = w_in.astype(BF16)
    for l in range(DEPTH):
        za, zb, zc, zg = _in_proj(x2, row(pre_norm[l]), w_in_bf, l)
        mu = jnp.concatenate([rwkv_mu_rkv[l].reshape(-1), rwkv_mu_wa[l].reshape(-1),
                              jnp.zeros((D_RWKV,), F32)]).reshape(1, N_A)
        rwkv_params = (mu, row(rwkv_w0[l]), rwkv_w2[l].astype(BF16), row(rwkv_a0[l]),
                       rwkv_a2[l].astype(BF16), row(rwkv_k_k[l]), row(rwkv_k_a[l]),
                       row(rwkv_r_k[l]), row(rwkv_ln_w[l]), row(rwkv_ln_b[l]))
        ya = _rwkv(za, batch, seq, rwkv_params)
        yb = _retention(zb, batch, seq)
        ys = _s5(zc, batch, seq, (s5_A_re[l], s5_A_im[l], s5_log_dt[l], s5_B_re[l], s5_B_im[l],
                                  s5_C_re[l], s5_C_im[l]))
        merge_params = (row(s5_D[l]), s5_glu_w[l].astype(BF16), row(s5_glu_b[l]),
                        w_proj_rwkv[l].astype(BF16), w_proj_ret[l].astype(BF16),
                        w_proj_s5[l].astype(BF16), b_merge[l].astype(F32),
                        w_out[l].astype(BF16), row(post_norm[l]))
        x2 = _merge(ya, yb, ys, zc, zg, x2, merge_params)
    return x2.reshape(batch, seq, D_MODEL)
```

```python
import functools
import math

import jax
import jax.numpy as jnp
from jax import lax
from jax.experimental import pallas as pl
from jax.experimental.pallas import tpu as pltpu

F32 = jnp.float32
BF16 = jnp.bfloat16

D_MODEL = 1024
DEPTH = 2
HEAD_DIM = 64
D_RWKV = 512
W_LORA = 64
A_LORA = 64
RWKV_LN_EPS = 64e-5
D_RET = 512
RET_HEADS = 8
RET_CHUNK = 128
ROPE_BASE = 10000.0
S5_GROUP = 16
S5_STATE = 64
D_S5 = 512
S5_GROUPS = D_S5 // S5_GROUP
EPS = 1e-6

N_A = 4 * D_RWKV + W_LORA + A_LORA
N_B = 4 * D_RET
N_C = 2 * D_S5
N_G = 3 * D_MODEL
D_IN = N_A + N_B + N_C + N_G

RWKV_CHUNK = 64
RWKV_BLOCK = 512
RWKV_GROUP = 4
RET_GROUP = 2
RET_BLOCK = 256
S5_CHUNK = 32
S5_SEQS = 2
S5_GROUPS_PER_STEP = 4
S5_PITCH = 40

VMEM_LIMIT = 56 * 1024 * 1024
LANES = 128


def _bdot(a, b):
    return jnp.dot(a.astype(BF16), b.astype(BF16), preferred_element_type=F32)


def _bdot_nt(a, b):
    return lax.dot_general(a.astype(BF16), b.astype(BF16), (((1,), (1,)), ((), ())),
                           preferred_element_type=F32)


def _bdot_tn(a, b):
    return lax.dot_general(a.astype(BF16), b.astype(BF16), (((0,), (0,)), ((), ())),
                           preferred_element_type=F32)


def _dot_split(m, x):
    hi = x.astype(BF16)
    lo = (x - hi.astype(F32)).astype(BF16)
    return (jnp.dot(m, hi, preferred_element_type=F32)
            + jnp.dot(m, lo, preferred_element_type=F32))


def _sigmoid(x):
    return 0.5 * jnp.tanh(0.5 * x) + 0.5


def _silu(x):
    return x * _sigmoid(x)


def _head_lane_masks(nheads):
    lane_head = jnp.arange(nheads * HEAD_DIM)[None, :] // HEAD_DIM
    return (lane_head == jnp.arange(nheads)[:, None]).astype(BF16)


def _head_sums(x, ones_bd):
    w = ones_bd.shape[0]
    return jnp.concatenate([_bdot(x[:, i:i + w], ones_bd) for i in range(0, x.shape[1], w)], axis=1)


def _block_diag(x, own):
    x = x.astype(BF16)
    return jnp.concatenate([x * own[h:h + 1, :] for h in range(own.shape[0])], axis=0)


def _in_proj_kernel(x_ref, g_ref, w_ref, oa_ref, ob_ref, oc_ref, og_ref):
    x = x_ref[...]
    h = (x * lax.rsqrt(jnp.mean(x * x, axis=-1, keepdims=True) + EPS) * g_ref[...]).astype(BF16)
    col = 0
    for o_ref in (oa_ref, ob_ref, oc_ref, og_ref):
        n = o_ref.shape[-1]
        for c0 in range(0, n, 512):
            c1 = min(c0 + 512, n)
            o_ref[:, c0:c1] = jnp.dot(h, w_ref[:, col + c0:col + c1],
                                      preferred_element_type=F32).astype(BF16)
        col += n


def _in_proj(x2, g, w, layer, tm=512):
    m = x2.shape[0]
    tm = min(tm, m)
    widths = (N_A, N_B, N_C, N_G)
    return pl.pallas_call(
        _in_proj_kernel,
        out_shape=[jax.ShapeDtypeStruct((m, n), BF16) for n in widths],
        grid=(m // tm,),
        in_specs=[pl.BlockSpec((tm, D_MODEL), lambda i: (i, 0)),
                  pl.BlockSpec((1, D_MODEL), lambda i: (0, 0)),
                  pl.BlockSpec((None, D_MODEL, D_IN), lambda i: (layer, 0, 0),
                               pipeline_mode=pl.Buffered(1))],
        out_specs=[pl.BlockSpec((tm, n), lambda i: (i, 0)) for n in widths],
        compiler_params=pltpu.CompilerParams(dimension_semantics=("arbitrary",),
                                             vmem_limit_bytes=VMEM_LIMIT),
        name="in_proj",
    )(x2, g, w)


def _rwkv_kernel(z_ref, mu_ref, w0_ref, w2_ref, a0_ref, a2_ref, kk_ref, ka_ref, rk_ref,
                 lnw_ref, lnb_ref, tri_ref, hsum_ref, own_ref, bd_ref,
                 o_ref, s_ref, prev_ref, y_ref):
    L, G, GW, NB = RWKV_CHUNK, RWKV_GROUP, RWKV_GROUP * HEAD_DIM, z_ref.shape[0]

    @pl.when(pl.program_id(1) == 0)
    def _():
        s_ref[...] = jnp.zeros_like(s_ref)
        prev_ref[...] = jnp.zeros_like(prev_ref)

    z = z_ref[...].astype(F32)
    row = lax.broadcasted_iota(jnp.int32, z.shape, 0)
    zp = jnp.where(row == 0, prev_ref[...], pltpu.roll(z, 1, 0))
    prev_ref[...] = z[NB - 1:NB, :]
    zs = z + mu_ref[...] * (zp - z)
    r = zs[:, 0:512]
    k = zs[:, 512:1024]
    v = zs[:, 1024:1536]
    xw = zs[:, 1536:1600]
    xa = zs[:, 1600:1664]
    gate = z[:, 1664:2176]

    t = -(w0_ref[...] + _bdot(jnp.tanh(xw), w2_ref[...]))
    softplus = jnp.maximum(t, 0.0) + jnp.log(1.0 + jnp.exp(-jnp.abs(t)))
    ld = -jnp.exp(-softplus - 0.5)
    a = _sigmoid(a0_ref[...] + _bdot(xa, a2_ref[...]))
    kk = k * kk_ref[...]
    k2 = k * (1.0 + (a - 1.0) * ka_ref[...])
    hsum = hsum_ref[...]
    kk = kk * lax.rsqrt(_head_sums(kk * kk, hsum) + 1e-12)
    b = kk * a

    tri = tri_ref[...]
    tb = tri.shape[0]
    cum = jnp.concatenate([_dot_split(tri, ld[i:i + tb]) for i in range(0, NB, tb)],
                          axis=0)
    tot = jnp.concatenate([jnp.broadcast_to(cum[c * L + L - 1:(c + 1) * L], (L, D_RWKV))
                           for c in range(NB // L)], axis=0)
    e_neg = jnp.exp(-cum)
    e_rem = jnp.exp(tot - cum)
    rt = r * jnp.exp(cum)
    at = -kk * jnp.exp(cum - ld)
    kt = k2 * e_neg
    bt = b * e_neg
    kh = k2 * e_rem
    bh = b * e_rem
    w_tot = jnp.exp(tot)

    t_idx = lax.broadcasted_iota(jnp.int32, (L, GW), 0)
    s_idx = lax.broadcasted_iota(jnp.int32, (L, GW), 1) % L
    strict = s_idx < t_idx
    incl = s_idx <= t_idx
    eye = jnp.where(s_idx == t_idx, 1.0, 0.0)
    own = own_ref[...]
    bd = bd_ref[...] != 0.0
    blk = lambda m: _block_diag(m, own)

    ngroups, nchunks = D_RWKV // GW, NB // L
    inst = [(slice(c * L, (c + 1) * L), slice(g * GW, (g + 1) * GW))
            for c in range(nchunks) for g in range(ngroups)]
    ars = [jnp.concatenate([at[i], rt[i]], axis=0) for i in inst]
    g_bs = [_bdot_nt(ar, blk(bt[i])) for ar, i in zip(ars, inst)]
    g_ks = [_bdot_nt(ar, blk(kt[i])) for ar, i in zip(ars, inst)]
    m_abs = [jnp.where(strict, m[0:L], 0.0) for m in g_bs]
    m_rbs = [jnp.where(incl, m[L:2 * L], 0.0) for m in g_bs]
    m_aks = [jnp.where(strict, m[0:L], 0.0) for m in g_ks]
    m_rks = [jnp.where(incl, m[L:2 * L], 0.0) for m in g_ks]
    ps = m_abs
    tinvs = [eye + p for p in ps]
    for _ in range(5):
        ps = [_bdot(p, blk(p)) for p in ps]
        tinvs = [t + _bdot(p, blk(t)) for p, t in zip(ps, tinvs)]
    v_bds = [blk(v[i]) for i in inst]
    tas = [_bdot(t, blk(at[i])) for t, i in zip(tinvs, inst)]
    mvs = [_bdot(m, vb) for m, vb in zip(m_aks, v_bds)]
    tmvs = [_bdot(t, blk(mv)) for t, mv in zip(tinvs, mvs)]
    ras = [rt[i] + _bdot(m, blk(ta)) for i, m, ta in zip(inst, m_rbs, tas)]
    y0s = [_bdot(mb, blk(tmv)) + _bdot(mk, vb)
           for mb, tmv, mk, vb in zip(m_rbs, tmvs, m_rks, v_bds)]
    qs = [jnp.where(bd, _bdot_tn(ta, bh[i]), 0.0) for ta, i in zip(tas, inst)]
    cmats = [jnp.where(bd, _bdot_tn(jnp.concatenate([tmv, v[i]], axis=0),
                                    jnp.concatenate([bh[i], kh[i]], axis=0)), 0.0)
             for tmv, i in zip(tmvs, inst)]
    states = [s_ref[g] for g in range(ngroups)]
    for j, (rs, sl) in enumerate(inst):
        g = j % ngroups
        s = states[g]
        y_ref[rs, sl] = _bdot_nt(ras[j], s) + y0s[j]
        states[g] = s * w_tot[rs.start:rs.start + 1, sl] + _bdot(s, qs[j]) + cmats[j]
    for g in range(ngroups):
        s_ref[g] = states[g]
    y = y_ref[...]

    d = y - _head_sums(y, hsum) * (1.0 / HEAD_DIM)
    var = _head_sums(d * d, hsum) * (1.0 / HEAD_DIM)
    yn = d * lax.rsqrt(var + RWKV_LN_EPS) * lnw_ref[...] + lnb_ref[...]
    bonus = _head_sums(r * k2 * rk_ref[...], hsum) * v
    o_ref[...] = ((yn + bonus) * _silu(gate)).astype(BF16)


def _rwkv(za, batch, seq, p):
    L, gw, nb = RWKV_CHUNK, RWKV_GROUP * HEAD_DIM, min(RWKV_BLOCK, seq)
    own = _head_lane_masks(RWKV_GROUP)
    gid = jnp.arange(gw) // HEAD_DIM
    bd = (gid[:, None] == gid[None, :]).astype(F32)
    bidx = jnp.arange(gw)
    chunk_same = (bidx[:, None] // L) == (bidx[None, :] // L)
    tri = (chunk_same & (bidx[None, :] <= bidx[:, None])).astype(BF16)
    hsum = bd.astype(BF16)
    consts = (tri, hsum, own, bd)
    full = lambda a: pl.BlockSpec(a.shape, lambda b, c: (0,) * a.ndim)
    nblk = seq // nb
    return pl.pallas_call(
        _rwkv_kernel,
        out_shape=jax.ShapeDtypeStruct((batch * seq, D_RWKV), BF16),
        grid=(batch, nblk),
        in_specs=[pl.BlockSpec((nb, N_A), lambda b, c: (b * nblk + c, 0))]
                 + [full(a) for a in p] + [full(a) for a in consts],
        out_specs=pl.BlockSpec((nb, D_RWKV), lambda b, c: (b * nblk + c, 0)),
        scratch_shapes=[pltpu.VMEM((D_RWKV // gw, gw, gw), F32),
                        pltpu.VMEM((1, N_A), F32),
                        pltpu.VMEM((nb, D_RWKV), F32)],
        compiler_params=pltpu.CompilerParams(dimension_semantics=("arbitrary", "arbitrary"),
                                             vmem_limit_bytes=VMEM_LIMIT),
        name="rwkv7",
    )(za, *p, *consts)


def _ret_kernel(z_ref, cos_ref, sin_ref, qw_ref, kw_ref, cd_ref, dmask_ref, own_ref, bd_ref,
                o_ref, s_ref):
    C, GW, NB = RET_CHUNK, RET_GROUP * HEAD_DIM, z_ref.shape[0]

    @pl.when(pl.program_id(1) == 0)
    def _():
        s_ref[...] = jnp.zeros_like(s_ref)

    z = z_ref[...].astype(F32)
    lane = lax.broadcasted_iota(jnp.int32, (NB, D_RET), 1)
    first_half = (lane % HEAD_DIM) < (HEAD_DIM // 2)
    cos = cos_ref[...]
    sin = sin_ref[...]

    def rope(x):
        swapped = jnp.where(first_half, pltpu.roll(x, D_RET - HEAD_DIM // 2, 1),
                            pltpu.roll(x, HEAD_DIM // 2, 1))
        return x * cos + swapped * sin

    q = rope(z[:, 0:512])
    k = rope(z[:, 512:1024]) * (HEAD_DIM ** -0.5)
    v = z[:, 1024:1536]
    gate = z[:, 1536:2048]
    cd = cd_ref[...]
    own = own_ref[...]
    bd = bd_ref[...] != 0.0

    ngroups = D_RET // GW
    inst = [(slice(c * C, (c + 1) * C), slice(g * GW, (g + 1) * GW), g)
            for c in range(NB // C) for g in range(ngroups)]
    scores = [_bdot_nt(q[rs, sl], _block_diag(k[rs, sl], own)) * dmask_ref[g]
              for rs, sl, g in inst]
    upd = [_bdot_tn(k[rs, sl] * kw_ref[:, sl], v[rs, sl]) for rs, sl, g in inst]
    intra = [_bdot(sc, _block_diag(v[rs, sl], own)) for sc, (rs, sl, g) in zip(scores, inst)]
    states = [s_ref[g] for g in range(ngroups)]
    ys = [[None] * ngroups for _ in range(NB // C)]
    for j, (rs, sl, g) in enumerate(inst):
        ys[j // ngroups][g] = intra[j] + _bdot(q[rs, sl] * qw_ref[:, sl], states[g])
        states[g] = states[g] * cd[:, sl] + jnp.where(bd, upd[j], 0.0)
    for g in range(ngroups):
        s_ref[g] = states[g]
    y = jnp.concatenate([jnp.concatenate(row, axis=1) for row in ys], axis=0)

    hsum = bd_ref[...].astype(BF16)
    d = y - _head_sums(y, hsum) * (1.0 / HEAD_DIM)
    var = _head_sums(d * d, hsum) * (1.0 / HEAD_DIM)
    o_ref[...] = (d * lax.rsqrt(var + EPS) * _silu(gate)).astype(BF16)


def _retention(zb, batch, seq):
    C, gw = RET_CHUNK, RET_GROUP * HEAD_DIM
    half = HEAD_DIM // 2
    inv = ROPE_BASE ** (-jnp.arange(half, dtype=F32) / half)
    ang = jnp.arange(seq, dtype=F32)[:, None] * inv[None, :]
    cos = jnp.tile(jnp.cos(ang), (1, 2 * RET_HEADS))
    sin = jnp.tile(jnp.concatenate([-jnp.sin(ang), jnp.sin(ang)], axis=1), (1, RET_HEADS))
    log_gamma = jnp.log(1.0 - 2.0 ** (-5.0 - jnp.arange(RET_HEADS, dtype=F32)))
    idx = jnp.arange(C, dtype=F32)
    diff = idx[:, None] - idx[None, :]
    dmask = jnp.where(diff >= 0, jnp.exp(log_gamma[:, None, None] * jnp.maximum(diff, 0.0)), 0.0)
    ngr = RET_HEADS // RET_GROUP
    dm = dmask.reshape(ngr, RET_GROUP, C, C).transpose(0, 2, 1, 3).reshape(ngr, C, RET_GROUP * C)
    own = _head_lane_masks(RET_GROUP)
    per_lane = lambda t: jnp.repeat(t, HEAD_DIM, axis=-1)
    qw = per_lane(jnp.exp(log_gamma[None, :] * (idx + 1.0)[:, None]))
    kw = per_lane(jnp.exp(log_gamma[None, :] * (C - 1.0 - idx)[:, None]))
    cd = per_lane(jnp.exp(log_gamma * C)[None, :])
    gid = jnp.arange(gw) // HEAD_DIM
    bd = (gid[:, None] == gid[None, :]).astype(F32)
    nb = min(RET_BLOCK, seq)
    nblk = seq // nb
    full = lambda a: pl.BlockSpec(a.shape, lambda b, c: (0,) * a.ndim)
    return pl.pallas_call(
        _ret_kernel,
        out_shape=jax.ShapeDtypeStruct((batch * seq, D_RET), BF16),
        grid=(batch, nblk),
        in_specs=[pl.BlockSpec((nb, N_B), lambda b, c: (b * nblk + c, 0)),
                  pl.BlockSpec((nb, D_RET), lambda b, c: (c, 0)),
                  pl.BlockSpec((nb, D_RET), lambda b, c: (c, 0)),
                  full(qw), full(kw), full(cd), full(dm), full(own), full(bd)],
        out_specs=pl.BlockSpec((nb, D_RET), lambda b, c: (b * nblk + c, 0)),
        scratch_shapes=[pltpu.VMEM((D_RET // gw, gw, gw), F32)],
        compiler_params=pltpu.CompilerParams(dimension_semantics=("arbitrary", "arbitrary"),
                                             vmem_limit_bytes=VMEM_LIMIT),
        name="retention",
    )(zb, cos, sin, qw, kw, cd, dm, own, bd)


def _s5_kernel(u_ref, tz_ref, so_ref, si_ref, lam_ref, o_ref, uf_ref, dt_ref, yt_ref, *, nc):
    lc, n = S5_CHUNK, S5_STATE
    ncb = dt_ref.shape[-1]
    g = pl.program_id(1)

    ntile = uf_ref.shape[0]
    gpt = LANES // S5_GROUP

    @pl.when(g == 0)
    def _():
        for j in range(ntile):
            for c in range(ncb):
                uf_ref[j, c * S5_PITCH:c * S5_PITCH + lc, :] = (
                    u_ref[c * lc:(c + 1) * lc, j * LANES:(j + 1) * LANES].astype(F32))
        for s in range(lc):
            for j in range(ntile):
                tok = uf_ref[j, pl.ds(s, ncb, stride=S5_PITCH), :]
                dt_ref[j * gpt:(j + 1) * gpt, s] = tok.T.astype(BF16).reshape(gpt, S5_GROUP, ncb)

    gs = range(tz_ref.shape[0])
    ds = [dt_ref[g * len(gs) + i].reshape(lc * S5_GROUP, ncb) for i in gs]
    ys = [jnp.dot(tz_ref[i], ds[i], preferred_element_type=F32) for i in gs]
    incs = [jnp.dot(so_ref[i], ds[i], preferred_element_type=F32) for i in gs]
    zs = [(inc[0:n], inc[n:2 * n]) for inc in incs]
    chunk = lax.broadcasted_iota(jnp.int32, (n, ncb), 1) % nc
    shifted = lambda z, sh: jnp.where(chunk >= sh, pltpu.roll(z, sh, 1), 0.0)
    for lvl in range(lam_ref.shape[1]):
        nxt = []
        for i, (zr, zi) in enumerate(zs):
            ar, ai = lam_ref[i, lvl, 0], lam_ref[i, lvl, 1]
            pr, pi = shifted(zr, 1 << lvl), shifted(zi, 1 << lvl)
            nxt.append((zr + ar * pr - ai * pi, zi + ar * pi + ai * pr))
        zs = nxt
    for i, (zr, zi) in enumerate(zs):
        x0 = jnp.concatenate([shifted(zr, 1), shifted(zi, 1)], axis=0)
        y = ys[i] + jnp.dot(si_ref[i], x0.astype(BF16), preferred_element_type=F32)
        yt_ref[g * len(gs) + i] = y.reshape(lc, S5_GROUP, ncb)

    @pl.when(g == pl.num_programs(1) - 1)
    def _():
        for t in range(lc):
            for j in range(ntile):
                yt = yt_ref[j * gpt:(j + 1) * gpt, t].reshape(LANES, ncb)
                uf_ref[j, pl.ds(t, ncb, stride=S5_PITCH), :] = yt.T
        for j in range(ntile):
            for c in range(ncb):
                o_ref[c * lc:(c + 1) * lc, j * LANES:(j + 1) * LANES] = (
                    uf_ref[j, c * S5_PITCH:c * S5_PITCH + lc, :].astype(BF16))


def _toeplitz_kernel(p_ref, o_ref):
    lc, pch = S5_CHUNK, S5_GROUP
    x = p_ref[0]
    width = x.shape[1]
    for t in range(lc):
        start = (lc - 1 - t) * pch
        w = x if start == 0 else pltpu.roll(x, width - start, 1)
        o_ref[0, t * pch:(t + 1) * pch, :] = w[:, 0:lc * pch].astype(BF16)


def _toeplitz(padded):
    groups, pch, width = padded.shape
    n = width // 2
    return pl.pallas_call(
        _toeplitz_kernel,
        out_shape=jax.ShapeDtypeStruct((groups, n, n), BF16),
        grid=(groups,),
        in_specs=[pl.BlockSpec((1, pch, width), lambda g: (g, 0, 0))],
        out_specs=pl.BlockSpec((1, n, n), lambda g: (g, 0, 0)),
        compiler_params=pltpu.CompilerParams(dimension_semantics=("arbitrary",)),
        name="s5_toeplitz",
    )(padded)


def _s5_tables(a_re, a_im, log_dt, b_re, b_im, c_re, c_im, levels, lanes):
    lc, n, pch = S5_CHUNK, S5_STATE, S5_GROUP
    dt = jnp.exp(log_dt)[:, None]
    ab_re = jnp.exp(dt * a_re) * jnp.cos(dt * a_im)
    ab_im = jnp.exp(dt * a_re) * jnp.sin(dt * a_im)
    pr, qi = ab_re - 1.0, ab_im
    den = a_re * a_re + a_im * a_im
    cr = (pr * a_re + qi * a_im) / den
    ci = (qi * a_re - pr * a_im) / den
    bb_re = cr[..., None] * b_re - ci[..., None] * b_im
    bb_im = cr[..., None] * b_im + ci[..., None] * b_re
    j = jnp.arange(lc + 1, dtype=F32)[:, None, None]
    pw_mag = jnp.exp(j * (dt * a_re)[None])
    pw_re = pw_mag * jnp.cos(j * (dt * a_im)[None])
    pw_im = pw_mag * jnp.sin(j * (dt * a_im)[None])
    cl_re = c_re[None] * pw_re[:, :, None, :] - c_im[None] * pw_im[:, :, None, :]
    cl_im = c_re[None] * pw_im[:, :, None, :] + c_im[None] * pw_re[:, :, None, :]
    kern = (jnp.einsum('jgpn,gnq->jgpq', cl_re, bb_re)
            - jnp.einsum('jgpn,gnq->jgpq', cl_im, bb_im))
    by_lag = kern[lc - 1::-1][:lc].transpose(1, 2, 0, 3).reshape(S5_GROUPS, pch, lc * pch)
    tz = _toeplitz(jnp.pad(by_lag, ((0, 0), (0, 0), (0, lc * pch))))
    rev_re = pw_re[lc - 1::-1][:lc]
    rev_im = pw_im[lc - 1::-1][:lc]
    so_re = rev_re[..., None] * bb_re[None] - rev_im[..., None] * bb_im[None]
    so_im = rev_re[..., None] * bb_im[None] + rev_im[..., None] * bb_re[None]
    so = jnp.concatenate([so_re, so_im], axis=2)
    so = so.transpose(1, 2, 0, 3).reshape(S5_GROUPS, 2 * n, lc * pch)
    si = jnp.concatenate([cl_re[1:], -cl_im[1:]], axis=3)
    si = si.transpose(1, 0, 2, 3).reshape(S5_GROUPS, lc * pch, 2 * n)
    e = (lc * 2.0 ** jnp.arange(levels, dtype=F32))[None, :, None]
    mag = jnp.exp(e * (dt * a_re)[:, None, :])
    lam = jnp.stack([mag * jnp.cos(e * (dt * a_im)[:, None, :]),
                     mag * jnp.sin(e * (dt * a_im)[:, None, :])], axis=2)
    lam = jnp.broadcast_to(lam[..., None], lam.shape + (lanes,))
    return tz.astype(BF16), so.astype(BF16), si.astype(BF16), lam


def _s5(zc, batch, seq, params):
    lc = S5_CHUNK
    nc = seq // lc
    nb = S5_SEQS if batch % S5_SEQS == 0 else 1
    ncb = nb * nc
    levels = max(1, (nc - 1).bit_length())
    tz, so, si, lam = _s5_tables(*params, levels, ncb)
    per_group = lambda a: pl.BlockSpec((S5_GROUPS_PER_STEP,) + a.shape[1:],
                                       lambda i, g: (g,) + (0,) * (a.ndim - 1))
    tokens = pl.BlockSpec((nb * seq, D_S5), lambda i, g: (i, 0))
    return pl.pallas_call(
        functools.partial(_s5_kernel, nc=nc),
        out_shape=jax.ShapeDtypeStruct((batch * seq, D_S5), BF16),
        grid=(batch // nb, S5_GROUPS // S5_GROUPS_PER_STEP),
        in_specs=[tokens, per_group(tz), per_group(so), per_group(si), per_group(lam)],
        out_specs=tokens,
        scratch_shapes=[pltpu.VMEM((D_S5 // LANES, ncb * S5_PITCH, LANES), F32),
                        pltpu.VMEM((S5_GROUPS, lc, S5_GROUP, ncb), BF16),
                        pltpu.VMEM((S5_GROUPS, lc, S5_GROUP, ncb), F32)],
        compiler_params=pltpu.CompilerParams(dimension_semantics=("arbitrary", "arbitrary"),
                                             vmem_limit_bytes=VMEM_LIMIT),
        name="s5",
    )(zc, tz, so, si, lam)


def _merge_kernel(ya_ref, yb_ref, ys_ref, zc_ref, zg_ref, x_ref, dskip_ref, gluw_ref, glub_ref,
                  wa_ref, wb_ref, wc_ref, bm_ref, wo_ref, pn_ref, o_ref):
    zc = zc_ref[...].astype(F32)
    y = ys_ref[...].astype(F32) + dskip_ref[...] * zc[:, 0:D_S5]
    gz = 0.5 * y * (1.0 + jnp.tanh(math.sqrt(2.0 / math.pi) * (y + 0.044715 * (y * y * y))))
    yc = gz * _sigmoid(_bdot(gz, gluw_ref[...]) + glub_ref[...]) * _silu(zc[:, D_S5:2 * D_S5])
    zg = zg_ref[...].astype(F32)
    bm = bm_ref[...]
    merged = (_sigmoid(zg[:, 0:D_MODEL] + bm[0:1, :])
              * jnp.dot(ya_ref[...], wa_ref[...], preferred_element_type=F32)
              + _sigmoid(zg[:, D_MODEL:2 * D_MODEL] + bm[1:2, :])
              * jnp.dot(yb_ref[...], wb_ref[...], preferred_element_type=F32)
              + _sigmoid(zg[:, 2 * D_MODEL:3 * D_MODEL] + bm[2:3, :]) * _bdot(yc, wc_ref[...]))
    out = _bdot(merged, wo_ref[...])
    normed = out * lax.rsqrt(jnp.mean(out * out, axis=-1, keepdims=True) + EPS) * pn_ref[...]
    o_ref[...] = x_ref[...] + normed


def _merge(ya, yb, ys, zc, zg, x2, params, layer, tm=512):
    m = x2.shape[0]
    tm = min(tm, m)
    tok = lambda n: pl.BlockSpec((tm, n), lambda i: (i, 0))
    full = lambda a: (pl.BlockSpec(a.shape, lambda i: (0, 0)) if a.ndim == 2 else
                      pl.BlockSpec((None,) + a.shape[1:], lambda i: (layer, 0, 0)))
    return pl.pallas_call(
        _merge_kernel,
        out_shape=jax.ShapeDtypeStruct((m, D_MODEL), F32),
        grid=(m // tm,),
        in_specs=[tok(D_RWKV), tok(D_RET), tok(D_S5), tok(N_C), tok(N_G), tok(D_MODEL)]
                 + [full(a) for a in params],
        out_specs=tok(D_MODEL),
        compiler_params=pltpu.CompilerParams(dimension_semantics=("arbitrary",),
                                             vmem_limit_bytes=VMEM_LIMIT),
        name="merge_out",
    )(ya, yb, ys, zc, zg, x2, *params)


def kernel(x, pre_norm, w_in, rwkv_mu_rkv, rwkv_mu_wa, rwkv_w0, rwkv_w2, rwkv_a0, rwkv_a2,
           rwkv_k_k, rwkv_k_a, rwkv_r_k, rwkv_ln_w, rwkv_ln_b, s5_A_re, s5_A_im, s5_log_dt,
           s5_B_re, s5_B_im, s5_C_re, s5_C_im, s5_D, s5_glu_w, s5_glu_b, w_proj_rwkv,
           w_proj_ret, w_proj_s5, b_merge, w_out, post_norm):
    batch, seq, _ = x.shape
    x2 = x.reshape(batch * seq, D_MODEL)
    row = lambda a: a.reshape(1, -1).astype(F32)
    w_in_bf = w_in.astype(BF16)
    glu_w_bf, w_out_bf = s5_glu_w.astype(BF16), w_out.astype(BF16)
    wp_a_bf, wp_b_bf, wp_c_bf = (w.astype(BF16) for w in (w_proj_rwkv, w_proj_ret, w_proj_s5))
    for l in range(DEPTH):
        za, zb, zc, zg = _in_proj(x2, row(pre_norm[l]), w_in_bf, l)
        mu = jnp.concatenate([rwkv_mu_rkv[l].reshape(-1), rwkv_mu_wa[l].reshape(-1),
                              jnp.zeros((D_RWKV,), F32)]).reshape(1, N_A)
        rwkv_params = (mu, row(rwkv_w0[l]), rwkv_w2[l].astype(BF16), row(rwkv_a0[l]),
                       rwkv_a2[l].astype(BF16), row(rwkv_k_k[l]), row(rwkv_k_a[l]),
                       row(rwkv_r_k[l]), row(rwkv_ln_w[l]), row(rwkv_ln_b[l]))
        ya = _rwkv(za, batch, seq, rwkv_params)
        yb = _retention(zb, batch, seq)
        ys = _s5(zc, batch, seq, (s5_A_re[l], s5_A_im[l], s5_log_dt[l], s5_B_re[l], s5_B_im[l],
                                  s5_C_re[l], s5_C_im[l]))
        merge_params = (row(s5_D[l]), glu_w_bf, row(s5_glu_b[l]), wp_a_bf, wp_b_bf, wp_c_bf,
                        b_merge[l].astype(F32), w_out_bf, row(post_norm[l]))
        x2 = _merge(ya, yb, ys, zc, zg, x2, merge_params, l)
    return x2.reshape(batch, seq, D_MODEL)
```

```python
import functools
import math

import jax
import jax.numpy as jnp
from jax import lax
from jax.experimental import pallas as pl
from jax.experimental.pallas import tpu as pltpu

F32 = jnp.float32
BF16 = jnp.bfloat16

D_MODEL = 1024
DEPTH = 2
HEAD_DIM = 64
D_RWKV = 512
W_LORA = 64
A_LORA = 64
RWKV_LN_EPS = 64e-5
D_RET = 512
RET_HEADS = 8
RET_CHUNK = 128
ROPE_BASE = 10000.0
S5_GROUP = 16
S5_STATE = 64
D_S5 = 512
S5_GROUPS = D_S5 // S5_GROUP
EPS = 1e-6

N_A = 4 * D_RWKV + W_LORA + A_LORA
N_B = 4 * D_RET
N_C = 2 * D_S5
N_G = 3 * D_MODEL
D_IN = N_A + N_B + N_C + N_G

RWKV_CHUNK = 64
RWKV_BLOCK = 512
RWKV_GROUP = 4
RET_GROUP = 2
RET_BLOCK = 256
S5_CHUNK = 32
S5_SEQS = 2
S5_GROUPS_PER_STEP = 4
S5_PITCH = 40

VMEM_LIMIT = 56 * 1024 * 1024
LANES = 128


def _bdot(a, b):
    return jnp.dot(a.astype(BF16), b.astype(BF16), preferred_element_type=F32)


def _bdot_nt(a, b):
    return lax.dot_general(a.astype(BF16), b.astype(BF16), (((1,), (1,)), ((), ())),
                           preferred_element_type=F32)


def _bdot_tn(a, b):
    return lax.dot_general(a.astype(BF16), b.astype(BF16), (((0,), (0,)), ((), ())),
                           preferred_element_type=F32)


def _dot_split(m, x):
    hi = x.astype(BF16)
    lo = (x - hi.astype(F32)).astype(BF16)
    return (jnp.dot(m, hi, preferred_element_type=F32)
            + jnp.dot(m, lo, preferred_element_type=F32))


def _sigmoid(x):
    return 0.5 * jnp.tanh(0.5 * x) + 0.5


def _silu(x):
    return x * _sigmoid(x)


def _head_lane_masks(nheads):
    lane_head = jnp.arange(nheads * HEAD_DIM)[None, :] // HEAD_DIM
    return (lane_head == jnp.arange(nheads)[:, None]).astype(BF16)


def _head_sums(x, ones_bd):
    w = ones_bd.shape[0]
    return jnp.concatenate([_bdot(x[:, i:i + w], ones_bd) for i in range(0, x.shape[1], w)], axis=1)


def _block_diag(x, own):
    x = x.astype(BF16)
    return jnp.concatenate([x * own[h:h + 1, :] for h in range(own.shape[0])], axis=0)


def _in_proj_kernel(x_ref, g_ref, w_ref, oa_ref, ob_ref, oc_ref, og_ref):
    x = x_ref[...]
    h = (x * lax.rsqrt(jnp.mean(x * x, axis=-1, keepdims=True) + EPS) * g_ref[...]).astype(BF16)
    col = 0
    for o_ref in (oa_ref, ob_ref, oc_ref, og_ref):
        n = o_ref.shape[-1]
        for c0 in range(0, n, 512):
            c1 = min(c0 + 512, n)
            o_ref[:, c0:c1] = jnp.dot(h, w_ref[:, col + c0:col + c1],
                                      preferred_element_type=F32).astype(BF16)
        col += n


def _in_proj(x2, g, w, layer, tm=512):
    m = x2.shape[0]
    tm = min(tm, m)
    widths = (N_A, N_B, N_C, N_G)
    return pl.pallas_call(
        _in_proj_kernel,
        out_shape=[jax.ShapeDtypeStruct((m, n), BF16) for n in widths],
        grid=(m // tm,),
        in_specs=[pl.BlockSpec((tm, D_MODEL), lambda i: (i, 0)),
                  pl.BlockSpec((1, D_MODEL), lambda i: (0, 0)),
                  pl.BlockSpec((None, D_MODEL, D_IN), lambda i: (layer, 0, 0),
                               pipeline_mode=pl.Buffered(1))],
        out_specs=[pl.BlockSpec((tm, n), lambda i: (i, 0)) for n in widths],
        compiler_params=pltpu.CompilerParams(dimension_semantics=("arbitrary",),
                                             vmem_limit_bytes=VMEM_LIMIT),
        name="in_proj",
    )(x2, g, w)


def _rwkv_kernel(z_ref, mu_ref, w0_ref, w2_ref, a0_ref, a2_ref, kk_ref, ka_ref, rk_ref,
                 lnw_ref, lnb_ref, tri_ref, hsum_ref, own_ref, bd_ref,
                 o_ref, s_ref, prev_ref, y_ref):
    L, G, GW, NB = RWKV_CHUNK, RWKV_GROUP, RWKV_GROUP * HEAD_DIM, z_ref.shape[0]

    @pl.when(pl.program_id(1) == 0)
    def _():
        s_ref[...] = jnp.zeros_like(s_ref)
        prev_ref[...] = jnp.zeros_like(prev_ref)

    z = z_ref[...].astype(F32)
    row = lax.broadcasted_iota(jnp.int32, z.shape, 0)
    zp = jnp.where(row == 0, prev_ref[...], pltpu.roll(z, 1, 0))
    prev_ref[...] = z[NB - 1:NB, :]
    zs = z + mu_ref[...] * (zp - z)
    r = zs[:, 0:512]
    k = zs[:, 512:1024]
    v = zs[:, 1024:1536]
    xw = zs[:, 1536:1600]
    xa = zs[:, 1600:1664]
    gate = z[:, 1664:2176]

    t = -(w0_ref[...] + _bdot(jnp.tanh(xw), w2_ref[...]))
    softplus = jnp.maximum(t, 0.0) + jnp.log(1.0 + jnp.exp(-jnp.abs(t)))
    ld = -jnp.exp(-softplus - 0.5)
    a = _sigmoid(a0_ref[...] + _bdot(xa, a2_ref[...]))
    kk = k * kk_ref[...]
    k2 = k * (1.0 + (a - 1.0) * ka_ref[...])
    hsum = hsum_ref[...]
    kk = kk * lax.rsqrt(_head_sums(kk * kk, hsum) + 1e-12)
    b = kk * a

    tri = tri_ref[...]
    tb = tri.shape[0]
    cum = jnp.concatenate([_dot_split(tri, ld[i:i + tb]) for i in range(0, NB, tb)],
                          axis=0)
    tot = jnp.concatenate([jnp.broadcast_to(cum[c * L + L - 1:(c + 1) * L], (L, D_RWKV))
                           for c in range(NB // L)], axis=0)
    e_neg = jnp.exp(-cum)
    e_rem = jnp.exp(tot - cum)
    rt = r * jnp.exp(cum)
    at = -kk * jnp.exp(cum - ld)
    kt = k2 * e_neg
    bt = b * e_neg
    kh = k2 * e_rem
    bh = b * e_rem
    w_tot = jnp.exp(tot)

    t_idx = lax.broadcasted_iota(jnp.int32, (L, GW), 0)
    s_idx = lax.broadcasted_iota(jnp.int32, (L, GW), 1) % L
    strict = s_idx < t_idx
    incl = s_idx <= t_idx
    eye = jnp.where(s_idx == t_idx, 1.0, 0.0)
    own = own_ref[...]
    bd = bd_ref[...] != 0.0
    blk = lambda m: _block_diag(m, own)

    ngroups, nchunks = D_RWKV // GW, NB // L
    inst = [(slice(c * L, (c + 1) * L), slice(g * GW, (g + 1) * GW))
            for c in range(nchunks) for g in range(ngroups)]
    ars = [jnp.concatenate([at[i], rt[i]], axis=0) for i in inst]
    g_bs = [_bdot_nt(ar, blk(bt[i])) for ar, i in zip(ars, inst)]
    g_ks = [_bdot_nt(ar, blk(kt[i])) for ar, i in zip(ars, inst)]
    m_abs = [jnp.where(strict, m[0:L], 0.0) for m in g_bs]
    m_rbs = [jnp.where(incl, m[L:2 * L], 0.0) for m in g_bs]
    m_aks = [jnp.where(strict, m[0:L], 0.0) for m in g_ks]
    m_rks = [jnp.where(incl, m[L:2 * L], 0.0) for m in g_ks]
    tinvs = [eye + p for p in m_abs]
    ps = [_bdot(p, blk(p)) for p in m_abs]
    for _ in range(4):
        prods = [_bdot(jnp.concatenate([t, p], axis=0), blk(p)) for t, p in zip(tinvs, ps)]
        tinvs = [t + pr[0:L] for t, pr in zip(tinvs, prods)]
        ps = [pr[L:2 * L] for pr in prods]
    tinvs = [t + _bdot(t, blk(p)) for t, p in zip(tinvs, ps)]
    v_bds = [blk(v[i]) for i in inst]
    tas = [_bdot(t, blk(at[i])) for t, i in zip(tinvs, inst)]
    mkvs = [_bdot(jnp.concatenate([ma, mr], axis=0), vb)
            for ma, mr, vb in zip(m_aks, m_rks, v_bds)]
    tmvs = [_bdot(t, blk(mkv[0:L])) for t, mkv in zip(tinvs, mkvs)]
    ras = [rt[i] + _bdot(m, blk(ta)) for i, m, ta in zip(inst, m_rbs, tas)]
    y0s = [_bdot(mb, blk(tmv)) + mkv[L:2 * L] for mb, tmv, mkv in zip(m_rbs, tmvs, mkvs)]
    qs = [jnp.where(bd, _bdot_tn(ta, bh[i]), 0.0) for ta, i in zip(tas, inst)]
    cmats = [jnp.where(bd, _bdot_tn(jnp.concatenate([tmv, v[i]], axis=0),
                                    jnp.concatenate([bh[i], kh[i]], axis=0)), 0.0)
             for tmv, i in zip(tmvs, inst)]
    states = [s_ref[g] for g in range(ngroups)]
    for j, (rs, sl) in enumerate(inst):
        g = j % ngroups
        s = states[g]
        y_ref[rs, sl] = _bdot_nt(ras[j], s) + y0s[j]
        states[g] = s * w_tot[rs.start:rs.start + 1, sl] + _bdot(s, qs[j]) + cmats[j]
    for g in range(ngroups):
        s_ref[g] = states[g]
    y = y_ref[...]

    d = y - _head_sums(y, hsum) * (1.0 / HEAD_DIM)
    var = _head_sums(d * d, hsum) * (1.0 / HEAD_DIM)
    yn = d * lax.rsqrt(var + RWKV_LN_EPS) * lnw_ref[...] + lnb_ref[...]
    bonus = _head_sums(r * k2 * rk_ref[...], hsum) * v
    o_ref[...] = ((yn + bonus) * _silu(gate)).astype(BF16)


def _rwkv(za, batch, seq, p):
    L, gw, nb = RWKV_CHUNK, RWKV_GROUP * HEAD_DIM, min(RWKV_BLOCK, seq)
    own = _head_lane_masks(RWKV_GROUP)
    gid = jnp.arange(gw) // HEAD_DIM
    bd = (gid[:, None] == gid[None, :]).astype(F32)
    bidx = jnp.arange(gw)
    chunk_same = (bidx[:, None] // L) == (bidx[None, :] // L)
    tri = (chunk_same & (bidx[None, :] <= bidx[:, None])).astype(BF16)
    hsum = bd.astype(BF16)
    consts = (tri, hsum, own, bd)
    full = lambda a: pl.BlockSpec(a.shape, lambda b, c: (0,) * a.ndim)
    nblk = seq // nb
    return pl.pallas_call(
        _rwkv_kernel,
        out_shape=jax.ShapeDtypeStruct((batch * seq, D_RWKV), BF16),
        grid=(batch, nblk),
        in_specs=[pl.BlockSpec((nb, N_A), lambda b, c: (b * nblk + c, 0))]
                 + [full(a) for a in p] + [full(a) for a in consts],
        out_specs=pl.BlockSpec((nb, D_RWKV), lambda b, c: (b * nblk + c, 0)),
        scratch_shapes=[pltpu.VMEM((D_RWKV // gw, gw, gw), F32),
                        pltpu.VMEM((1, N_A), F32),
                        pltpu.VMEM((nb, D_RWKV), F32)],
        compiler_params=pltpu.CompilerParams(dimension_semantics=("arbitrary", "arbitrary"),
                                             vmem_limit_bytes=VMEM_LIMIT),
        name="rwkv7",
    )(za, *p, *consts)


def _ret_kernel(z_ref, cos_ref, sin_ref, qw_ref, kw_ref, cd_ref, dmask_ref, own_ref, bd_ref,
                o_ref, s_ref):
    C, GW, NB = RET_CHUNK, RET_GROUP * HEAD_DIM, z_ref.shape[0]

    @pl.when(pl.program_id(1) == 0)
    def _():
        s_ref[...] = jnp.zeros_like(s_ref)

    z = z_ref[...].astype(F32)
    lane = lax.broadcasted_iota(jnp.int32, (NB, D_RET), 1)
    first_half = (lane % HEAD_DIM) < (HEAD_DIM // 2)
    cos = cos_ref[...]
    sin = sin_ref[...]

    def rope(x):
        swapped = jnp.where(first_half, pltpu.roll(x, D_RET - HEAD_DIM // 2, 1),
                            pltpu.roll(x, HEAD_DIM // 2, 1))
        return x * cos + swapped * sin

    q = rope(z[:, 0:512])
    k = rope(z[:, 512:1024]) * (HEAD_DIM ** -0.5)
    v = z[:, 1024:1536]
    gate = z[:, 1536:2048]
    cd = cd_ref[...]
    own = own_ref[...]
    bd = bd_ref[...] != 0.0

    ngroups = D_RET // GW
    inst = [(slice(c * C, (c + 1) * C), slice(g * GW, (g + 1) * GW), g)
            for c in range(NB // C) for g in range(ngroups)]
    scores = [_bdot_nt(q[rs, sl], _block_diag(k[rs, sl], own)) * dmask_ref[g]
              for rs, sl, g in inst]
    upd = [_bdot_tn(k[rs, sl] * kw_ref[:, sl], v[rs, sl]) for rs, sl, g in inst]
    intra = [_bdot(sc, _block_diag(v[rs, sl], own)) for sc, (rs, sl, g) in zip(scores, inst)]
    states = [s_ref[g] for g in range(ngroups)]
    ys = [[None] * ngroups for _ in range(NB // C)]
    for j, (rs, sl, g) in enumerate(inst):
        ys[j // ngroups][g] = intra[j] + _bdot(q[rs, sl] * qw_ref[:, sl], states[g])
        states[g] = states[g] * cd[:, sl] + jnp.where(bd, upd[j], 0.0)
    for g in range(ngroups):
        s_ref[g] = states[g]
    y = jnp.concatenate([jnp.concatenate(row, axis=1) for row in ys], axis=0)

    hsum = bd_ref[...].astype(BF16)
    d = y - _head_sums(y, hsum) * (1.0 / HEAD_DIM)
    var = _head_sums(d * d, hsum) * (1.0 / HEAD_DIM)
    o_ref[...] = (d * lax.rsqrt(var + EPS) * _silu(gate)).astype(BF16)


def _retention(zb, batch, seq):
    C, gw = RET_CHUNK, RET_GROUP * HEAD_DIM
    half = HEAD_DIM // 2
    inv = ROPE_BASE ** (-jnp.arange(half, dtype=F32) / half)
    ang = jnp.arange(seq, dtype=F32)[:, None] * inv[None, :]
    cos = jnp.tile(jnp.cos(ang), (1, 2 * RET_HEADS))
    sin = jnp.tile(jnp.concatenate([-jnp.sin(ang), jnp.sin(ang)], axis=1), (1, RET_HEADS))
    log_gamma = jnp.log(1.0 - 2.0 ** (-5.0 - jnp.arange(RET_HEADS, dtype=F32)))
    idx = jnp.arange(C, dtype=F32)
    diff = idx[:, None] - idx[None, :]
    dmask = jnp.where(diff >= 0, jnp.exp(log_gamma[:, None, None] * jnp.maximum(diff, 0.0)), 0.0)
    ngr = RET_HEADS // RET_GROUP
    dm = dmask.reshape(ngr, RET_GROUP, C, C).transpose(0, 2, 1, 3).reshape(ngr, C, RET_GROUP * C)
    own = _head_lane_masks(RET_GROUP)
    per_lane = lambda t: jnp.repeat(t, HEAD_DIM, axis=-1)
    qw = per_lane(jnp.exp(log_gamma[None, :] * (idx + 1.0)[:, None]))
    kw = per_lane(jnp.exp(log_gamma[None, :] * (C - 1.0 - idx)[:, None]))
    cd = per_lane(jnp.exp(log_gamma * C)[None, :])
    gid = jnp.arange(gw) // HEAD_DIM
    bd = (gid[:, None] == gid[None, :]).astype(F32)
    nb = min(RET_BLOCK, seq)
    nblk = seq // nb
    full = lambda a: pl.BlockSpec(a.shape, lambda b, c: (0,) * a.ndim)
    return pl.pallas_call(
        _ret_kernel,
        out_shape=jax.ShapeDtypeStruct((batch * seq, D_RET), BF16),
        grid=(batch, nblk),
        in_specs=[pl.BlockSpec((nb, N_B), lambda b, c: (b * nblk + c, 0)),
                  pl.BlockSpec((nb, D_RET), lambda b, c: (c, 0)),
                  pl.BlockSpec((nb, D_RET), lambda b, c: (c, 0)),
                  full(qw), full(kw), full(cd), full(dm), full(own), full(bd)],
        out_specs=pl.BlockSpec((nb, D_RET), lambda b, c: (b * nblk + c, 0)),
        scratch_shapes=[pltpu.VMEM((D_RET // gw, gw, gw), F32)],
        compiler_params=pltpu.CompilerParams(dimension_semantics=("arbitrary", "arbitrary"),
                                             vmem_limit_bytes=VMEM_LIMIT),
        name="retention",
    )(zb, cos, sin, qw, kw, cd, dm, own, bd)


def _s5_kernel(u_ref, tz_ref, so_ref, si_ref, lam_ref, o_ref, uf_ref, dt_ref, yt_ref, *, nc):
    lc, n = S5_CHUNK, S5_STATE
    ncb = dt_ref.shape[-1]
    g = pl.program_id(1)

    ntile = uf_ref.shape[0]
    gpt = LANES // S5_GROUP

    @pl.when(g == 0)
    def _():
        for j in range(ntile):
            for c in range(ncb):
                uf_ref[j, c * S5_PITCH:c * S5_PITCH + lc, :] = (
                    u_ref[c * lc:(c + 1) * lc, j * LANES:(j + 1) * LANES].astype(F32))
        for s in range(lc):
            for j in range(ntile):
                tok = uf_ref[j, pl.ds(s, ncb, stride=S5_PITCH), :]
                dt_ref[j * gpt:(j + 1) * gpt, s] = tok.T.astype(BF16).reshape(gpt, S5_GROUP, ncb)

    gs = range(tz_ref.shape[0])
    ds = [dt_ref[g * len(gs) + i].reshape(lc * S5_GROUP, ncb) for i in gs]
    ys = [jnp.dot(tz_ref[i], ds[i], preferred_element_type=F32) for i in gs]
    incs = [jnp.dot(so_ref[i], ds[i], preferred_element_type=F32) for i in gs]
    zs = [(inc[0:n], inc[n:2 * n]) for inc in incs]
    chunk = lax.broadcasted_iota(jnp.int32, (n, ncb), 1) % nc
    shifted = lambda z, sh: jnp.where(chunk >= sh, pltpu.roll(z, sh, 1), 0.0)
    for lvl in range(lam_ref.shape[1]):
        nxt = []
        for i, (zr, zi) in enumerate(zs):
            ar, ai = lam_ref[i, lvl, 0], lam_ref[i, lvl, 1]
            pr, pi = shifted(zr, 1 << lvl), shifted(zi, 1 << lvl)
            nxt.append((zr + ar * pr - ai * pi, zi + ar * pi + ai * pr))
        zs = nxt
    for i, (zr, zi) in enumerate(zs):
        x0 = jnp.concatenate([shifted(zr, 1), shifted(zi, 1)], axis=0)
        y = ys[i] + jnp.dot(si_ref[i], x0.astype(BF16), preferred_element_type=F32)
        yt_ref[g * len(gs) + i] = y.reshape(lc, S5_GROUP, ncb)

    @pl.when(g == pl.num_programs(1) - 1)
    def _():
        for t in range(lc):
            for j in range(ntile):
                yt = yt_ref[j * gpt:(j + 1) * gpt, t].reshape(LANES, ncb)
                uf_ref[j, pl.ds(t, ncb, stride=S5_PITCH), :] = yt.T
        for j in range(ntile):
            for c in range(ncb):
                o_ref[c * lc:(c + 1) * lc, j * LANES:(j + 1) * LANES] = (
                    uf_ref[j, c * S5_PITCH:c * S5_PITCH + lc, :].astype(BF16))


def _toeplitz_kernel(p_ref, o_ref):
    lc, pch = S5_CHUNK, S5_GROUP
    x = p_ref[0]
    width = x.shape[1]
    for t in range(lc):
        start = (lc - 1 - t) * pch
        w = x if start == 0 else pltpu.roll(x, width - start, 1)
        o_ref[0, t * pch:(t + 1) * pch, :] = w[:, 0:lc * pch].astype(BF16)


def _toeplitz(padded):
    groups, pch, width = padded.shape
    n = width // 2
    return pl.pallas_call(
        _toeplitz_kernel,
        out_shape=jax.ShapeDtypeStruct((groups, n, n), BF16),
        grid=(groups,),
        in_specs=[pl.BlockSpec((1, pch, width), lambda g: (g, 0, 0))],
        out_specs=pl.BlockSpec((1, n, n), lambda g: (g, 0, 0)),
        compiler_params=pltpu.CompilerParams(dimension_semantics=("arbitrary",)),
        name="s5_toeplitz",
    )(padded)


def _s5_tables(a_re, a_im, log_dt, b_re, b_im, c_re, c_im, levels, lanes):
    lc, n, pch = S5_CHUNK, S5_STATE, S5_GROUP
    dt = jnp.exp(log_dt)[:, None]
    ab_re = jnp.exp(dt * a_re) * jnp.cos(dt * a_im)
    ab_im = jnp.exp(dt * a_re) * jnp.sin(dt * a_im)
    pr, qi = ab_re - 1.0, ab_im
    den = a_re * a_re + a_im * a_im
    cr = (pr * a_re + qi * a_im) / den
    ci = (qi * a_re - pr * a_im) / den
    bb_re = cr[..., None] * b_re - ci[..., None] * b_im
    bb_im = cr[..., None] * b_im + ci[..., None] * b_re
    j = jnp.arange(lc + 1, dtype=F32)[:, None, None]
    pw_mag = jnp.exp(j * (dt * a_re)[None])
    pw_re = pw_mag * jnp.cos(j * (dt * a_im)[None])
    pw_im = pw_mag * jnp.sin(j * (dt * a_im)[None])
    cl_re = c_re[None] * pw_re[:, :, None, :] - c_im[None] * pw_im[:, :, None, :]
    cl_im = c_re[None] * pw_im[:, :, None, :] + c_im[None] * pw_re[:, :, None, :]
    kern = (jnp.einsum('jgpn,gnq->jgpq', cl_re, bb_re)
            - jnp.einsum('jgpn,gnq->jgpq', cl_im, bb_im))
    by_lag = kern[lc - 1::-1][:lc].transpose(1, 2, 0, 3).reshape(S5_GROUPS, pch, lc * pch)
    tz = _toeplitz(jnp.pad(by_lag, ((0, 0), (0, 0), (0, lc * pch))))
    rev_re = pw_re[lc - 1::-1][:lc]
    rev_im = pw_im[lc - 1::-1][:lc]
    so_re = rev_re[..., None] * bb_re[None] - rev_im[..., None] * bb_im[None]
    so_im = rev_re[..., None] * bb_im[None] + rev_im[..., None] * bb_re[None]
    so = jnp.concatenate([so_re, so_im], axis=2)
    so = so.transpose(1, 2, 0, 3).reshape(S5_GROUPS, 2 * n, lc * pch)
    si = jnp.concatenate([cl_re[1:], -cl_im[1:]], axis=3)
    si = si.transpose(1, 0, 2, 3).reshape(S5_GROUPS, lc * pch, 2 * n)
    e = (lc * 2.0 ** jnp.arange(levels, dtype=F32))[None, :, None]
    mag = jnp.exp(e * (dt * a_re)[:, None, :])
    lam = jnp.stack([mag * jnp.cos(e * (dt * a_im)[:, None, :]),
                     mag * jnp.sin(e * (dt * a_im)[:, None, :])], axis=2)
    lam = jnp.broadcast_to(lam[..., None], lam.shape + (lanes,))
    return tz.astype(BF16), so.astype(BF16), si.astype(BF16), lam


def _s5(zc, batch, seq, params):
    lc = S5_CHUNK
    nc = seq // lc
    nb = S5_SEQS if batch % S5_SEQS == 0 else 1
    ncb = nb * nc
    levels = max(1, (nc - 1).bit_length())
    tz, so, si, lam = _s5_tables(*params, levels, ncb)
    per_group = lambda a: pl.BlockSpec((S5_GROUPS_PER_STEP,) + a.shape[1:],
                                       lambda i, g: (g,) + (0,) * (a.ndim - 1))
    tokens = pl.BlockSpec((nb * seq, D_S5), lambda i, g: (i, 0))
    return pl.pallas_call(
        functools.partial(_s5_kernel, nc=nc),
        out_shape=jax.ShapeDtypeStruct((batch * seq, D_S5), BF16),
        grid=(batch // nb, S5_GROUPS // S5_GROUPS_PER_STEP),
        in_specs=[tokens, per_group(tz), per_group(so), per_group(si), per_group(lam)],
        out_specs=tokens,
        scratch_shapes=[pltpu.VMEM((D_S5 // LANES, ncb * S5_PITCH, LANES), F32),
                        pltpu.VMEM((S5_GROUPS, lc, S5_GROUP, ncb), BF16),
                        pltpu.VMEM((S5_GROUPS, lc, S5_GROUP, ncb), F32)],
        compiler_params=pltpu.CompilerParams(dimension_semantics=("arbitrary", "arbitrary"),
                                             vmem_limit_bytes=VMEM_LIMIT),
        name="s5",
    )(zc, tz, so, si, lam)


def _merge_kernel(ya_ref, yb_ref, ys_ref, zc_ref, zg_ref, x_ref, dskip_ref, gluw_ref, glub_ref,
                  wa_ref, wb_ref, wc_ref, bm_ref, wo_ref, pn_ref, o_ref):
    zc = zc_ref[...].astype(F32)
    y = ys_ref[...].astype(F32) + dskip_ref[...] * zc[:, 0:D_S5]
    gz = 0.5 * y * (1.0 + jnp.tanh(math.sqrt(2.0 / math.pi) * (y + 0.044715 * (y * y * y))))
    yc = gz * _sigmoid(_bdot(gz, gluw_ref[...]) + glub_ref[...]) * _silu(zc[:, D_S5:2 * D_S5])
    zg = zg_ref[...].astype(F32)
    bm = bm_ref[...]
    merged = (_sigmoid(zg[:, 0:D_MODEL] + bm[0:1, :])
              * jnp.dot(ya_ref[...], wa_ref[...], preferred_element_type=F32)
              + _sigmoid(zg[:, D_MODEL:2 * D_MODEL] + bm[1:2, :])
              * jnp.dot(yb_ref[...], wb_ref[...], preferred_element_type=F32)
              + _sigmoid(zg[:, 2 * D_MODEL:3 * D_MODEL] + bm[2:3, :]) * _bdot(yc, wc_ref[...]))
    out = _bdot(merged, wo_ref[...])
    normed = out * lax.rsqrt(jnp.mean(out * out, axis=-1, keepdims=True) + EPS) * pn_ref[...]
    o_ref[...] = x_ref[...] + normed


def _merge(ya, yb, ys, zc, zg, x2, params, layer, tm=512):
    m = x2.shape[0]
    tm = min(tm, m)
    tok = lambda n: pl.BlockSpec((tm, n), lambda i: (i, 0))
    full = lambda a: (pl.BlockSpec(a.shape, lambda i: (0, 0)) if a.ndim == 2 else
                      pl.BlockSpec((None,) + a.shape[1:], lambda i: (layer, 0, 0)))
    return pl.pallas_call(
        _merge_kernel,
        out_shape=jax.ShapeDtypeStruct((m, D_MODEL), F32),
        grid=(m // tm,),
        in_specs=[tok(D_RWKV), tok(D_RET), tok(D_S5), tok(N_C), tok(N_G), tok(D_MODEL)]
                 + [full(a) for a in params],
        out_specs=tok(D_MODEL),
        compiler_params=pltpu.CompilerParams(dimension_semantics=("arbitrary",),
                                             vmem_limit_bytes=VMEM_LIMIT),
        name="merge_out",
    )(ya, yb, ys, zc, zg, x2, *params)


def kernel(x, pre_norm, w_in, rwkv_mu_rkv, rwkv_mu_wa, rwkv_w0, rwkv_w2, rwkv_a0, rwkv_a2,
           rwkv_k_k, rwkv_k_a, rwkv_r_k, rwkv_ln_w, rwkv_ln_b, s5_A_re, s5_A_im, s5_log_dt,
           s5_B_re, s5_B_im, s5_C_re, s5_C_im, s5_D, s5_glu_w, s5_glu_b, w_proj_rwkv,
           w_proj_ret, w_proj_s5, b_merge, w_out, post_norm):
    batch, seq, _ = x.shape
    x2 = x.reshape(batch * seq, D_MODEL)
    row = lambda a: a.reshape(1, -1).astype(F32)
    w_in_bf = w_in.astype(BF16)
    glu_w_bf, w_out_bf = s5_glu_w.astype(BF16), w_out.astype(BF16)
    wp_a_bf, wp_b_bf, wp_c_bf = (w.astype(BF16) for w in (w_proj_rwkv, w_proj_ret, w_proj_s5))
    for l in range(DEPTH):
        za, zb, zc, zg = _in_proj(x2, row(pre_norm[l]), w_in_bf, l)
        mu = jnp.concatenate([rwkv_mu_rkv[l].reshape(-1), rwkv_mu_wa[l].reshape(-1),
                              jnp.zeros((D_RWKV,), F32)]).reshape(1, N_A)
        rwkv_params = (mu, row(rwkv_w0[l]), rwkv_w2[l].astype(BF16), row(rwkv_a0[l]),
                       rwkv_a2[l].astype(BF16), row(rwkv_k_k[l]), row(rwkv_k_a[l]),
                       row(rwkv_r_k[l]), row(rwkv_ln_w[l]), row(rwkv_ln_b[l]))
        ya = _rwkv(za, batch, seq, rwkv_params)
        yb = _retention(zb, batch, seq)
        ys = _s5(zc, batch, seq, (s5_A_re[l], s5_A_im[l], s5_log_dt[l], s5_B_re[l], s5_B_im[l],
                                  s5_C_re[l], s5_C_im[l]))
        merge_params = (row(s5_D[l]), glu_w_bf, row(s5_glu_b[l]), wp_a_bf, wp_b_bf, wp_c_bf,
                        b_merge[l].astype(F32), w_out_bf, row(post_norm[l]))
        x2 = _merge(ya, yb, ys, zc, zg, x2, merge_params, l)
    return x2.reshape(batch, seq, D_MODEL)
```

```python
import functools
import math

import jax
import jax.numpy as jnp
from jax import lax
from jax.experimental import pallas as pl
from jax.experimental.pallas import tpu as pltpu

F32 = jnp.float32
BF16 = jnp.bfloat16

D_MODEL = 1024
DEPTH = 2
HEAD_DIM = 64
D_RWKV = 512
W_LORA = 64
A_LORA = 64
RWKV_LN_EPS = 64e-5
D_RET = 512
RET_HEADS = 8
RET_CHUNK = 128
ROPE_BASE = 10000.0
S5_GROUP = 16
S5_STATE = 64
D_S5 = 512
S5_GROUPS = D_S5 // S5_GROUP
EPS = 1e-6

N_A = 4 * D_RWKV + W_LORA + A_LORA
N_B = 4 * D_RET
N_C = 2 * D_S5
N_G = 3 * D_MODEL
D_IN = N_A + N_B + N_C + N_G

RWKV_CHUNK = 64
RWKV_BLOCK = 512
RWKV_GROUP = 4
RET_GROUP = 2
RET_BLOCK = 512
S5_CHUNK = 32
S5_SEQS = 2
S5_GROUPS_PER_STEP = 4
S5_PITCH = 40

VMEM_LIMIT = 56 * 1024 * 1024
LANES = 128


def _bdot(a, b):
    return jnp.dot(a.astype(BF16), b.astype(BF16), preferred_element_type=F32)


def _bdot_nt(a, b):
    return lax.dot_general(a.astype(BF16), b.astype(BF16), (((1,), (1,)), ((), ())),
                           preferred_element_type=F32)


def _bdot_tn(a, b):
    return lax.dot_general(a.astype(BF16), b.astype(BF16), (((0,), (0,)), ((), ())),
                           preferred_element_type=F32)


def _dot_split(m, x):
    hi = x.astype(BF16)
    lo = (x - hi.astype(F32)).astype(BF16)
    return (jnp.dot(m, hi, preferred_element_type=F32)
            + jnp.dot(m, lo, preferred_element_type=F32))


def _sigmoid(x):
    return 0.5 * jnp.tanh(0.5 * x) + 0.5


def _silu(x):
    return x * _sigmoid(x)


def _head_lane_masks(nheads):
    lane_head = jnp.arange(nheads * HEAD_DIM)[None, :] // HEAD_DIM
    return (lane_head == jnp.arange(nheads)[:, None]).astype(BF16)


def _head_sums(x, ones_bd):
    w = ones_bd.shape[0]
    return jnp.concatenate([_bdot(x[:, i:i + w], ones_bd) for i in range(0, x.shape[1], w)], axis=1)


def _block_diag(x, own):
    x = x.astype(BF16)
    return jnp.concatenate([x * own[h:h + 1, :] for h in range(own.shape[0])], axis=0)


def _in_proj_kernel(x_ref, g_ref, w_ref, oa_ref, ob_ref, oc_ref, og_ref):
    x = x_ref[...]
    h = (x * lax.rsqrt(jnp.mean(x * x, axis=-1, keepdims=True) + EPS) * g_ref[...]).astype(BF16)
    col = 0
    for o_ref in (oa_ref, ob_ref, oc_ref, og_ref):
        n = o_ref.shape[-1]
        for c0 in range(0, n, 512):
            c1 = min(c0 + 512, n)
            o_ref[:, c0:c1] = jnp.dot(h, w_ref[:, col + c0:col + c1],
                                      preferred_element_type=F32).astype(BF16)
        col += n


def _in_proj(x2, g, w, layer, tm=512):
    m = x2.shape[0]
    tm = min(tm, m)
    widths = (N_A, N_B, N_C, N_G)
    return pl.pallas_call(
        _in_proj_kernel,
        out_shape=[jax.ShapeDtypeStruct((m, n), BF16) for n in widths],
        grid=(m // tm,),
        in_specs=[pl.BlockSpec((tm, D_MODEL), lambda i: (i, 0)),
                  pl.BlockSpec((1, D_MODEL), lambda i: (0, 0)),
                  pl.BlockSpec((None, D_MODEL, D_IN), lambda i: (layer, 0, 0),
                               pipeline_mode=pl.Buffered(1))],
        out_specs=[pl.BlockSpec((tm, n), lambda i: (i, 0)) for n in widths],
        compiler_params=pltpu.CompilerParams(dimension_semantics=("arbitrary",),
                                             vmem_limit_bytes=VMEM_LIMIT),
        name="in_proj",
    )(x2, g, w)


def _rwkv_kernel(z_ref, mu_ref, w0_ref, w2_ref, a0_ref, a2_ref, kk_ref, ka_ref, rk_ref,
                 lnw_ref, lnb_ref, tri_ref, hsum_ref, own_ref, bd_ref,
                 o_ref, s_ref, prev_ref, y_ref):
    L, G, GW, NB = RWKV_CHUNK, RWKV_GROUP, RWKV_GROUP * HEAD_DIM, z_ref.shape[0]

    @pl.when(pl.program_id(1) == 0)
    def _():
        s_ref[...] = jnp.zeros_like(s_ref)
        prev_ref[...] = jnp.zeros_like(prev_ref)

    n_shift = N_A - D_RWKV
    z = z_ref[:, 0:n_shift].astype(F32)
    gate = z_ref[:, n_shift:N_A].astype(F32)
    row = lax.broadcasted_iota(jnp.int32, z.shape, 0)
    zp = jnp.where(row == 0, prev_ref[...], pltpu.roll(z, 1, 0))
    prev_ref[...] = z[NB - 1:NB, :]
    zs = z + mu_ref[...] * (zp - z)
    r = zs[:, 0:512]
    k = zs[:, 512:1024]
    v = zs[:, 1024:1536]
    xw = zs[:, 1536:1600]
    xa = zs[:, 1600:1664]

    t = -(w0_ref[...] + _bdot(jnp.tanh(xw), w2_ref[...]))
    softplus = jnp.maximum(t, 0.0) + jnp.log(1.0 + jnp.exp(-jnp.abs(t)))
    ld = -jnp.exp(-softplus - 0.5)
    a = _sigmoid(a0_ref[...] + _bdot(xa, a2_ref[...]))
    kk = k * kk_ref[...]
    k2 = k * (1.0 + (a - 1.0) * ka_ref[...])
    hsum = hsum_ref[...]
    kk = kk * lax.rsqrt(_head_sums(kk * kk, hsum) + 1e-12)
    b = kk * a

    tri = tri_ref[...]
    tb = tri.shape[0]
    cum = jnp.concatenate([_dot_split(tri, ld[i:i + tb]) for i in range(0, NB, tb)],
                          axis=0)
    w_chunk = [jnp.exp(cum[c * L + L - 1:(c + 1) * L]) for c in range(NB // L)]
    e_neg = jnp.exp(-cum)
    e_rem = jnp.concatenate([e_neg[c * L:(c + 1) * L] * w_chunk[c] for c in range(NB // L)],
                            axis=0)
    rt = r * jnp.exp(cum)
    at = -kk * jnp.exp(cum - ld)
    kt = k2 * e_neg
    bt = b * e_neg
    kh = k2 * e_rem
    bh = b * e_rem

    t_idx = lax.broadcasted_iota(jnp.int32, (L, GW), 0)
    s_idx = lax.broadcasted_iota(jnp.int32, (L, GW), 1) % L
    strict = s_idx < t_idx
    incl = s_idx <= t_idx
    eye = jnp.where(s_idx == t_idx, 1.0, 0.0)
    own = own_ref[...]
    bd = bd_ref[...] != 0.0
    blk = lambda m: _block_diag(m, own)

    ngroups, nchunks = D_RWKV // GW, NB // L
    inst = [(slice(c * L, (c + 1) * L), slice(g * GW, (g + 1) * GW))
            for c in range(nchunks) for g in range(ngroups)]
    ars = [jnp.concatenate([at[i], rt[i]], axis=0) for i in inst]
    g_bs = [_bdot_nt(ar, blk(bt[i])) for ar, i in zip(ars, inst)]
    g_ks = [_bdot_nt(ar, blk(kt[i])) for ar, i in zip(ars, inst)]
    m_abs = [jnp.where(strict, m[0:L], 0.0) for m in g_bs]
    m_rbs = [jnp.where(incl, m[L:2 * L], 0.0) for m in g_bs]
    m_aks = [jnp.where(strict, m[0:L], 0.0) for m in g_ks]
    m_rks = [jnp.where(incl, m[L:2 * L], 0.0) for m in g_ks]
    tinvs = [eye + p for p in m_abs]
    ps = [_bdot(p, blk(p)) for p in m_abs]
    for _ in range(4):
        prods = [_bdot(jnp.concatenate([t, p], axis=0), blk(p)) for t, p in zip(tinvs, ps)]
        tinvs = [t + pr[0:L] for t, pr in zip(tinvs, prods)]
        ps = [pr[L:2 * L] for pr in prods]
    tinvs = [t + _bdot(t, blk(p)) for t, p in zip(tinvs, ps)]
    v_bds = [blk(v[i]) for i in inst]
    mkvs = [_bdot(jnp.concatenate([ma, mr], axis=0), vb)
            for ma, mr, vb in zip(m_aks, m_rks, v_bds)]
    tws = [jnp.concatenate([t, _bdot(m, blk(t))], axis=0) for t, m in zip(tinvs, m_rbs)]
    taws = [_bdot(tw, blk(at[i])) for tw, i in zip(tws, inst)]
    tmws = [_bdot(tw, blk(mkv[0:L])) for tw, mkv in zip(tws, mkvs)]
    tas = [x[0:L] for x in taws]
    tmvs = [x[0:L] for x in tmws]
    ras = [rt[i] + x[L:2 * L] for i, x in zip(inst, taws)]
    y0s = [x[L:2 * L] + mkv[L:2 * L] for x, mkv in zip(tmws, mkvs)]
    qs = [jnp.where(bd, _bdot_tn(ta, bh[i]), 0.0) for ta, i in zip(tas, inst)]
    cmats = [jnp.where(bd, _bdot_tn(jnp.concatenate([tmv, v[i]], axis=0),
                                    jnp.concatenate([bh[i], kh[i]], axis=0)), 0.0)
             for tmv, i in zip(tmvs, inst)]
    states = [s_ref[g] for g in range(ngroups)]
    for j, (rs, sl) in enumerate(inst):
        g = j % ngroups
        s = states[g]
        y_ref[rs, sl] = _bdot_nt(ras[j], s) + y0s[j]
        states[g] = s * w_chunk[j // ngroups][:, sl] + _bdot(s, qs[j]) + cmats[j]
    for g in range(ngroups):
        s_ref[g] = states[g]
    y = y_ref[...]

    d = y - _head_sums(y, hsum) * (1.0 / HEAD_DIM)
    var = _head_sums(d * d, hsum) * (1.0 / HEAD_DIM)
    yn = d * lax.rsqrt(var + RWKV_LN_EPS) * lnw_ref[...] + lnb_ref[...]
    bonus = _head_sums(r * k2 * rk_ref[...], hsum) * v
    o_ref[...] = ((yn + bonus) * _silu(gate)).astype(BF16)


def _rwkv(za, batch, seq, p):
    L, gw, nb = RWKV_CHUNK, RWKV_GROUP * HEAD_DIM, min(RWKV_BLOCK, seq)
    own = _head_lane_masks(RWKV_GROUP)
    gid = jnp.arange(gw) // HEAD_DIM
    bd = (gid[:, None] == gid[None, :]).astype(F32)
    bidx = jnp.arange(gw)
    chunk_same = (bidx[:, None] // L) == (bidx[None, :] // L)
    tri = (chunk_same & (bidx[None, :] <= bidx[:, None])).astype(BF16)
    hsum = bd.astype(BF16)
    consts = (tri, hsum, own, bd)
    full = lambda a: pl.BlockSpec(a.shape, lambda b, c: (0,) * a.ndim)
    nblk = seq // nb
    return pl.pallas_call(
        _rwkv_kernel,
        out_shape=jax.ShapeDtypeStruct((batch * seq, D_RWKV), BF16),
        grid=(batch, nblk),
        in_specs=[pl.BlockSpec((nb, N_A), lambda b, c: (b * nblk + c, 0))]
                 + [full(a) for a in p] + [full(a) for a in consts],
        out_specs=pl.BlockSpec((nb, D_RWKV), lambda b, c: (b * nblk + c, 0)),
        scratch_shapes=[pltpu.VMEM((D_RWKV // gw, gw, gw), F32),
                        pltpu.VMEM((1, N_A - D_RWKV), F32),
                        pltpu.VMEM((nb, D_RWKV), F32)],
        compiler_params=pltpu.CompilerParams(dimension_semantics=("arbitrary", "arbitrary"),
                                             vmem_limit_bytes=VMEM_LIMIT),
        name="rwkv7",
    )(za, *p, *consts)


def _ret_kernel(z_ref, cos_ref, sin_ref, qw_ref, kw_ref, cd_ref, dmask_ref, own_ref, bd_ref,
                o_ref, s_ref):
    C, GW, NB = RET_CHUNK, RET_GROUP * HEAD_DIM, z_ref.shape[0]

    @pl.when(pl.program_id(1) == 0)
    def _():
        s_ref[...] = jnp.zeros_like(s_ref)

    z = z_ref[...].astype(F32)
    lane = lax.broadcasted_iota(jnp.int32, (NB, D_RET), 1)
    first_half = (lane % HEAD_DIM) < (HEAD_DIM // 2)
    cos = cos_ref[...]
    sin = sin_ref[...]

    def rope(x):
        swapped = jnp.where(first_half, pltpu.roll(x, D_RET - HEAD_DIM // 2, 1),
                            pltpu.roll(x, HEAD_DIM // 2, 1))
        return x * cos + swapped * sin

    q = rope(z[:, 0:512])
    k = rope(z[:, 512:1024]) * (HEAD_DIM ** -0.5)
    v = z[:, 1024:1536]
    gate = z[:, 1536:2048]
    cd = cd_ref[...]
    own = own_ref[...]
    bd = bd_ref[...] != 0.0

    ngroups = D_RET // GW
    inst = [(slice(c * C, (c + 1) * C), slice(g * GW, (g + 1) * GW), g)
            for c in range(NB // C) for g in range(ngroups)]
    scores = [_bdot_nt(q[rs, sl], _block_diag(k[rs, sl], own)) * dmask_ref[g]
              for rs, sl, g in inst]
    upd = [_bdot_tn(k[rs, sl] * kw_ref[:, sl], v[rs, sl]) for rs, sl, g in inst]
    intra = [_bdot(sc, _block_diag(v[rs, sl], own)) for sc, (rs, sl, g) in zip(scores, inst)]
    states = [s_ref[g] for g in range(ngroups)]
    ys = [[None] * ngroups for _ in range(NB // C)]
    for j, (rs, sl, g) in enumerate(inst):
        ys[j // ngroups][g] = intra[j] + _bdot(q[rs, sl] * qw_ref[:, sl], states[g])
        states[g] = states[g] * cd[:, sl] + jnp.where(bd, upd[j], 0.0)
    for g in range(ngroups):
        s_ref[g] = states[g]
    y = jnp.concatenate([jnp.concatenate(row, axis=1) for row in ys], axis=0)

    hsum = bd_ref[...].astype(BF16)
    d = y - _head_sums(y, hsum) * (1.0 / HEAD_DIM)
    var = _head_sums(d * d, hsum) * (1.0 / HEAD_DIM)
    o_ref[...] = (d * lax.rsqrt(var + EPS) * _silu(gate)).astype(BF16)


def _retention(zb, batch, seq):
    C, gw = RET_CHUNK, RET_GROUP * HEAD_DIM
    half = HEAD_DIM // 2
    inv = ROPE_BASE ** (-jnp.arange(half, dtype=F32) / half)
    ang = jnp.arange(seq, dtype=F32)[:, None] * inv[None, :]
    cos = jnp.tile(jnp.cos(ang), (1, 2 * RET_HEADS))
    sin = jnp.tile(jnp.concatenate([-jnp.sin(ang), jnp.sin(ang)], axis=1), (1, RET_HEADS))
    log_gamma = jnp.log(1.0 - 2.0 ** (-5.0 - jnp.arange(RET_HEADS, dtype=F32)))
    idx = jnp.arange(C, dtype=F32)
    diff = idx[:, None] - idx[None, :]
    dmask = jnp.where(diff >= 0, jnp.exp(log_gamma[:, None, None] * jnp.maximum(diff, 0.0)), 0.0)
    ngr = RET_HEADS // RET_GROUP
    dm = dmask.reshape(ngr, RET_GROUP, C, C).transpose(0, 2, 1, 3).reshape(ngr, C, RET_GROUP * C)
    own = _head_lane_masks(RET_GROUP)
    per_lane = lambda t: jnp.repeat(t, HEAD_DIM, axis=-1)
    qw = per_lane(jnp.exp(log_gamma[None, :] * (idx + 1.0)[:, None]))
    kw = per_lane(jnp.exp(log_gamma[None, :] * (C - 1.0 - idx)[:, None]))
    cd = per_lane(jnp.exp(log_gamma * C)[None, :])
    gid = jnp.arange(gw) // HEAD_DIM
    bd = (gid[:, None] == gid[None, :]).astype(F32)
    nb = min(RET_BLOCK, seq)
    nblk = seq // nb
    full = lambda a: pl.BlockSpec(a.shape, lambda b, c: (0,) * a.ndim)
    return pl.pallas_call(
        _ret_kernel,
        out_shape=jax.ShapeDtypeStruct((batch * seq, D_RET), BF16),
        grid=(batch, nblk),
        in_specs=[pl.BlockSpec((nb, N_B), lambda b, c: (b * nblk + c, 0)),
                  pl.BlockSpec((nb, D_RET), lambda b, c: (c, 0)),
                  pl.BlockSpec((nb, D_RET), lambda b, c: (c, 0)),
                  full(qw), full(kw), full(cd), full(dm), full(own), full(bd)],
        out_specs=pl.BlockSpec((nb, D_RET), lambda b, c: (b * nblk + c, 0)),
        scratch_shapes=[pltpu.VMEM((D_RET // gw, gw, gw), F32)],
        compiler_params=pltpu.CompilerParams(dimension_semantics=("arbitrary", "arbitrary"),
                                             vmem_limit_bytes=VMEM_LIMIT),
        name="retention",
    )(zb, cos, sin, qw, kw, cd, dm, own, bd)


def _s5_kernel(u_ref, tz_ref, so_ref, si_ref, lam_ref, o_ref, uf_ref, dt_ref, yt_ref, *, nc):
    lc, n = S5_CHUNK, S5_STATE
    ncb = dt_ref.shape[-1]
    g = pl.program_id(1)

    ntile = uf_ref.shape[0]
    gpt = LANES // S5_GROUP

    @pl.when(g == 0)
    def _():
        for j in range(ntile):
            for c in range(ncb):
                uf_ref[j, c * S5_PITCH:c * S5_PITCH + lc, :] = (
                    u_ref[c * lc:(c + 1) * lc, j * LANES:(j + 1) * LANES].astype(F32))
        for s in range(lc):
            for j in range(ntile):
                tok = uf_ref[j, pl.ds(s, ncb, stride=S5_PITCH), :]
                dt_ref[j * gpt:(j + 1) * gpt, s] = tok.T.astype(BF16).reshape(gpt, S5_GROUP, ncb)

    gs = range(tz_ref.shape[0])
    ds = [dt_ref[g * len(gs) + i].reshape(lc * S5_GROUP, ncb) for i in gs]
    ys = [jnp.dot(tz_ref[i], ds[i], preferred_element_type=F32) for i in gs]
    incs = [jnp.dot(so_ref[i], ds[i], preferred_element_type=F32) for i in gs]
    zs = [(inc[0:n], inc[n:2 * n]) for inc in incs]
    chunk = lax.broadcasted_iota(jnp.int32, (n, ncb), 1) % nc
    shifted = lambda z, sh: jnp.where(chunk >= sh, pltpu.roll(z, sh, 1), 0.0)
    for lvl in range(lam_ref.shape[1]):
        nxt = []
        for i, (zr, zi) in enumerate(zs):
            ar, ai = lam_ref[i, lvl, 0], lam_ref[i, lvl, 1]
            pr, pi = shifted(zr, 1 << lvl), shifted(zi, 1 << lvl)
            nxt.append((zr + ar * pr - ai * pi, zi + ar * pi + ai * pr))
        zs = nxt
    for i, (zr, zi) in enumerate(zs):
        x0 = jnp.concatenate([shifted(zr, 1), shifted(zi, 1)], axis=0)
        y = ys[i] + jnp.dot(si_ref[i], x0.astype(BF16), preferred_element_type=F32)
        yt_ref[g * len(gs) + i] = y.reshape(lc, S5_GROUP, ncb)

    @pl.when(g == pl.num_programs(1) - 1)
    def _():
        for t in range(lc):
            for j in range(ntile):
                yt = yt_ref[j * gpt:(j + 1) * gpt, t].reshape(LANES, ncb)
                uf_ref[j, pl.ds(t, ncb, stride=S5_PITCH), :] = yt.T
        for j in range(ntile):
            for c in range(ncb):
                o_ref[c * lc:(c + 1) * lc, j * LANES:(j + 1) * LANES] = (
                    uf_ref[j, c * S5_PITCH:c * S5_PITCH + lc, :].astype(BF16))


def _toeplitz_kernel(p_ref, o_ref):
    lc, pch = S5_CHUNK, S5_GROUP
    x = p_ref[0]
    width = x.shape[1]
    for t in range(lc):
        start = (lc - 1 - t) * pch
        w = x if start == 0 else pltpu.roll(x, width - start, 1)
        o_ref[0, t * pch:(t + 1) * pch, :] = w[:, 0:lc * pch].astype(BF16)


def _toeplitz(padded):
    groups, pch, width = padded.shape
    n = width // 2
    return pl.pallas_call(
        _toeplitz_kernel,
        out_shape=jax.ShapeDtypeStruct((groups, n, n), BF16),
        grid=(groups,),
        in_specs=[pl.BlockSpec((1, pch, width), lambda g: (g, 0, 0))],
        out_specs=pl.BlockSpec((1, n, n), lambda g: (g, 0, 0)),
        compiler_params=pltpu.CompilerParams(dimension_semantics=("arbitrary",)),
        name="s5_toeplitz",
    )(padded)


def _s5_tables(a_re, a_im, log_dt, b_re, b_im, c_re, c_im, levels, lanes):
    lc, n, pch = S5_CHUNK, S5_STATE, S5_GROUP
    dt = jnp.exp(log_dt)[:, None]
    ab_re = jnp.exp(dt * a_re) * jnp.cos(dt * a_im)
    ab_im = jnp.exp(dt * a_re) * jnp.sin(dt * a_im)
    pr, qi = ab_re - 1.0, ab_im
    den = a_re * a_re + a_im * a_im
    cr = (pr * a_re + qi * a_im) / den
    ci = (qi * a_re - pr * a_im) / den
    bb_re = cr[..., None] * b_re - ci[..., None] * b_im
    bb_im = cr[..., None] * b_im + ci[..., None] * b_re
    j = jnp.arange(lc + 1, dtype=F32)[:, None, None]
    pw_mag = jnp.exp(j * (dt * a_re)[None])
    pw_re = pw_mag * jnp.cos(j * (dt * a_im)[None])
    pw_im = pw_mag * jnp.sin(j * (dt * a_im)[None])
    cl_re = c_re[None] * pw_re[:, :, None, :] - c_im[None] * pw_im[:, :, None, :]
    cl_im = c_re[None] * pw_im[:, :, None, :] + c_im[None] * pw_re[:, :, None, :]
    kern = (jnp.einsum('jgpn,gnq->jgpq', cl_re, bb_re)
            - jnp.einsum('jgpn,gnq->jgpq', cl_im, bb_im))
    by_lag = kern[lc - 1::-1][:lc].transpose(1, 2, 0, 3).reshape(S5_GROUPS, pch, lc * pch)
    tz = _toeplitz(jnp.pad(by_lag, ((0, 0), (0, 0), (0, lc * pch))))
    rev_re = pw_re[lc - 1::-1][:lc]
    rev_im = pw_im[lc - 1::-1][:lc]
    so_re = rev_re[..., None] * bb_re[None] - rev_im[..., None] * bb_im[None]
    so_im = rev_re[..., None] * bb_im[None] + rev_im[..., None] * bb_re[None]
    so = jnp.concatenate([so_re, so_im], axis=2)
    so = so.transpose(1, 2, 0, 3).reshape(S5_GROUPS, 2 * n, lc * pch)
    si = jnp.concatenate([cl_re[1:], -cl_im[1:]], axis=3)
    si = si.transpose(1, 0, 2, 3).reshape(S5_GROUPS, lc * pch, 2 * n)
    e = (lc * 2.0 ** jnp.arange(levels, dtype=F32))[None, :, None]
    mag = jnp.exp(e * (dt * a_re)[:, None, :])
    lam = jnp.stack([mag * jnp.cos(e * (dt * a_im)[:, None, :]),
                     mag * jnp.sin(e * (dt * a_im)[:, None, :])], axis=2)
    lam = jnp.broadcast_to(lam[..., None], lam.shape + (lanes,))
    return tz.astype(BF16), so.astype(BF16), si.astype(BF16), lam


def _s5(zc, batch, seq, params):
    lc = S5_CHUNK
    nc = seq // lc
    nb = S5_SEQS if batch % S5_SEQS == 0 else 1
    ncb = nb * nc
    levels = max(1, (nc - 1).bit_length())
    tz, so, si, lam = _s5_tables(*params, levels, ncb)
    per_group = lambda a: pl.BlockSpec((S5_GROUPS_PER_STEP,) + a.shape[1:],
                                       lambda i, g: (g,) + (0,) * (a.ndim - 1))
    tokens = pl.BlockSpec((nb * seq, D_S5), lambda i, g: (i, 0))
    return pl.pallas_call(
        functools.partial(_s5_kernel, nc=nc),
        out_shape=jax.ShapeDtypeStruct((batch * seq, D_S5), BF16),
        grid=(batch // nb, S5_GROUPS // S5_GROUPS_PER_STEP),
        in_specs=[tokens, per_group(tz), per_group(so), per_group(si), per_group(lam)],
        out_specs=tokens,
        scratch_shapes=[pltpu.VMEM((D_S5 // LANES, ncb * S5_PITCH, LANES), F32),
                        pltpu.VMEM((S5_GROUPS, lc, S5_GROUP, ncb), BF16),
                        pltpu.VMEM((S5_GROUPS, lc, S5_GROUP, ncb), F32)],
        compiler_params=pltpu.CompilerParams(dimension_semantics=("arbitrary", "arbitrary"),
                                             vmem_limit_bytes=VMEM_LIMIT),
        name="s5",
    )(zc, tz, so, si, lam)


def _merge_kernel(ya_ref, yb_ref, ys_ref, zc_ref, zg_ref, x_ref, dskip_ref, gluw_ref, glub_ref,
                  wa_ref, wb_ref, wc_ref, bm_ref, wo_ref, pn_ref, o_ref):
    zc = zc_ref[...].astype(F32)
    y = ys_ref[...].astype(F32) + dskip_ref[...] * zc[:, 0:D_S5]
    gz = 0.5 * y * (1.0 + jnp.tanh(math.sqrt(2.0 / math.pi) * (y + 0.044715 * (y * y * y))))
    yc = gz * _sigmoid(_bdot(gz, gluw_ref[...]) + glub_ref[...]) * _silu(zc[:, D_S5:2 * D_S5])
    zg = zg_ref[...].astype(F32)
    bm = bm_ref[...]
    merged = (_sigmoid(zg[:, 0:D_MODEL] + bm[0:1, :])
              * jnp.dot(ya_ref[...], wa_ref[...], preferred_element_type=F32)
              + _sigmoid(zg[:, D_MODEL:2 * D_MODEL] + bm[1:2, :])
              * jnp.dot(yb_ref[...], wb_ref[...], preferred_element_type=F32)
              + _sigmoid(zg[:, 2 * D_MODEL:3 * D_MODEL] + bm[2:3, :]) * _bdot(yc, wc_ref[...]))
    out = _bdot(merged, wo_ref[...])
    normed = out * lax.rsqrt(jnp.mean(out * out, axis=-1, keepdims=True) + EPS) * pn_ref[...]
    o_ref[...] = x_ref[...] + normed


def _merge(ya, yb, ys, zc, zg, x2, params, layer, tm=512):
    m = x2.shape[0]
    tm = min(tm, m)
    tok = lambda n: pl.BlockSpec((tm, n), lambda i: (i, 0))
    full = lambda a: (pl.BlockSpec(a.shape, lambda i: (0, 0)) if a.ndim == 2 else
                      pl.BlockSpec((None,) + a.shape[1:], lambda i: (layer, 0, 0)))
    return pl.pallas_call(
        _merge_kernel,
        out_shape=jax.ShapeDtypeStruct((m, D_MODEL), F32),
        grid=(m // tm,),
        in_specs=[tok(D_RWKV), tok(D_RET), tok(D_S5), tok(N_C), tok(N_G), tok(D_MODEL)]
                 + [full(a) for a in params],
        out_specs=tok(D_MODEL),
        compiler_params=pltpu.CompilerParams(dimension_semantics=("arbitrary",),
                                             vmem_limit_bytes=VMEM_LIMIT),
        name="merge_out",
    )(ya, yb, ys, zc, zg, x2, *params)


def kernel(x, pre_norm, w_in, rwkv_mu_rkv, rwkv_mu_wa, rwkv_w0, rwkv_w2, rwkv_a0, rwkv_a2,
           rwkv_k_k, rwkv_k_a, rwkv_r_k, rwkv_ln_w, rwkv_ln_b, s5_A_re, s5_A_im, s5_log_dt,
           s5_B_re, s5_B_im, s5_C_re, s5_C_im, s5_D, s5_glu_w, s5_glu_b, w_proj_rwkv,
           w_proj_ret, w_proj_s5, b_merge, w_out, post_norm):
    batch, seq, _ = x.shape
    x2 = x.reshape(batch * seq, D_MODEL)
    row = lambda a: a.reshape(1, -1).astype(F32)
    w_in_bf = w_in.astype(BF16)
    glu_w_bf, w_out_bf = s5_glu_w.astype(BF16), w_out.astype(BF16)
    wp_a_bf, wp_b_bf, wp_c_bf = (w.astype(BF16) for w in (w_proj_rwkv, w_proj_ret, w_proj_s5))
    for l in range(DEPTH):
        za, zb, zc, zg = _in_proj(x2, row(pre_norm[l]), w_in_bf, l)
        mu = jnp.concatenate([rwkv_mu_rkv[l].reshape(-1), rwkv_mu_wa[l].reshape(-1)]).reshape(1, -1)
        rwkv_params = (mu, row(rwkv_w0[l]), rwkv_w2[l].astype(BF16), row(rwkv_a0[l]),
                       rwkv_a2[l].astype(BF16), row(rwkv_k_k[l]), row(rwkv_k_a[l]),
                       row(rwkv_r_k[l]), row(rwkv_ln_w[l]), row(rwkv_ln_b[l]))
        ya = _rwkv(za, batch, seq, rwkv_params)
        yb = _retention(zb, batch, seq)
        ys = _s5(zc, batch, seq, (s5_A_re[l], s5_A_im[l], s5_log_dt[l], s5_B_re[l], s5_B_im[l],
                                  s5_C_re[l], s5_C_im[l]))
        merge_params = (row(s5_D[l]), glu_w_bf, row(s5_glu_b[l]), wp_a_bf, wp_b_bf, wp_c_bf,
                        b_merge[l].astype(F32), w_out_bf, row(post_norm[l]))
        x2 = _merge(ya, yb, ys, zc, zg, x2, merge_params, l)
    return x2.reshape(batch, seq, D_MODEL)
```
